```python
import math
import jax, jax.numpy as jnp
from jax import lax
import numpy as np

D_MODEL = 1024
BATCH = 4
SEQ = 4096
DEPTH = 1

SB_HEADS = 8
SB_HEAD_DIM = 64
SB_WIDTH = SB_HEADS * SB_HEAD_DIM
Q_BLOCK = 128
CONV_WIDTH = D_MODEL // 2
CONV_K = 3
MEM_LEN = 256
MEM_HEADS = 4
MEM_HEAD_DIM = D_MODEL // MEM_HEADS
FFN_HIDDEN = -(-8 * D_MODEL // (3 * 256)) * 256
EPS = 1e-6

IN_SPLITS = (SB_WIDTH, SB_WIDTH, SB_WIDTH, CONV_WIDTH, CONV_WIDTH, CONV_WIDTH, D_MODEL, D_MODEL)
IN_WIDTH = sum(IN_SPLITS)

kernel_name = "hybrid_stickbreak_shortconv_memxattn_swiglu"


def rms_norm(x, g):
    xf = x.astype(jnp.float32)
    var = jnp.mean(xf * xf, axis=-1, keepdims=True)
    return (xf * lax.rsqrt(var + EPS) * g.astype(jnp.float32)).astype(x.dtype)


def stick_breaking_attention(q, k, v):
    seq = q.shape[2]
    scale = 1.0 / math.sqrt(q.shape[-1])
    outs = []
    for blk in range(seq // Q_BLOCK):
        t0 = blk * Q_BLOCK
        n_keys = t0 + Q_BLOCK
        qb = q[:, :, t0:n_keys].astype(jnp.float32)
        kb = k[:, :, :n_keys].astype(jnp.float32)
        vb = v[:, :, :n_keys].astype(jnp.float32)
        z = jnp.einsum('bhqd,bhkd->bhqk', qb, kb) * scale
        q_pos = t0 + jnp.arange(Q_BLOCK)[:, None]
        k_pos = jnp.arange(n_keys)[None, :]
        mask = k_pos < q_pos
        log_fail = jnp.where(mask, jax.nn.log_sigmoid(-z), 0.0)
        log_later = lax.cumsum(log_fail, axis=3, reverse=True) - log_fail
        weights = jnp.where(mask, jnp.exp(jax.nn.log_sigmoid(z) + log_later), 0.0)
        outs.append(jnp.einsum('bhqk,bhkd->bhqd', weights, vb))
    return jnp.concatenate(outs, axis=2).astype(v.dtype)


def causal_depthwise_conv(u, w):
    c = u.shape[-1]
    return lax.conv_general_dilated(
        u, w[:, None, :].astype(u.dtype), window_strides=(1,), padding=((CONV_K - 1, 0),),
        dimension_numbers=('NWC', 'WIO', 'NWC'), feature_group_count=c)


def memory_cross_attention(h, m, w_q, w_kv, w_o):
    b, s, _ = h.shape
    mlen = m.shape[1]
    q = (h @ w_q).reshape(b, s, MEM_HEADS, MEM_HEAD_DIM)
    kv = (m @ w_kv).reshape(b, mlen, 2, MEM_HEADS, MEM_HEAD_DIM)
    k, v = kv[:, :, 0], kv[:, :, 1]
    scores = jnp.einsum('bshd,bmhd->bhsm', q.astype(jnp.float32), k.astype(jnp.float32))
    probs = jax.nn.softmax(scores / math.sqrt(MEM_HEAD_DIM), axis=-1)
    o = jnp.einsum('bhsm,bmhd->bshd', probs, v.astype(jnp.float32)).astype(h.dtype)
    return o.reshape(b, s, D_MODEL) @ w_o


def setup_inputs(seed: int = 0) -> dict:
    key = jax.random.key(seed)
    ks = jax.random.split(key, 20)
    f32 = jnp.float32

    def w(k, shape, fan_in):
        return jax.random.normal(k, shape, f32) * (fan_in ** -0.5)

    def gain(k, shape):
        return 1.0 + 0.05 * jax.random.normal(k, shape, f32)

    return {
        "x": jax.random.normal(ks[0], (BATCH, SEQ, D_MODEL), f32),
        "mem": jax.random.normal(ks[1], (BATCH, MEM_LEN, D_MODEL), f32),
        "norm_mix": gain(ks[2], (DEPTH, D_MODEL)),
        "w_in": w(ks[3], (DEPTH, D_MODEL, IN_WIDTH), D_MODEL),
        "conv_w": w(ks[4], (DEPTH, CONV_K, CONV_WIDTH), CONV_K),
        "w_branch_a": w(ks[5], (DEPTH, SB_WIDTH, D_MODEL), SB_WIDTH),
        "w_branch_b": w(ks[6], (DEPTH, CONV_WIDTH, D_MODEL), CONV_WIDTH),
        "w_mix_out": w(ks[7], (DEPTH, D_MODEL, D_MODEL), D_MODEL),
        "norm_mem_q": gain(ks[8], (DEPTH, D_MODEL)),
        "norm_mem_kv": gain(ks[9], (DEPTH, D_MODEL)),
        "w_mem_q": w(ks[10], (DEPTH, D_MODEL, D_MODEL), D_MODEL),
        "w_mem_kv": w(ks[11], (DEPTH, D_MODEL, 2 * D_MODEL), D_MODEL),
        "w_mem_o": w(ks[12], (DEPTH, D_MODEL, D_MODEL), D_MODEL),
        "norm_ffn": gain(ks[13], (DEPTH, D_MODEL)),
        "w_ffn_in": w(ks[14], (DEPTH, D_MODEL, 2 * FFN_HIDDEN), D_MODEL),
        "w_ffn_out": w(ks[15], (DEPTH, FFN_HIDDEN, D_MODEL), FFN_HIDDEN),
        "norm_final": gain(ks[16], (D_MODEL,)),
    }


def reference(x, mem, norm_mix, w_in, conv_w, w_branch_a, w_branch_b, w_mix_out,
              norm_mem_q, norm_mem_kv, w_mem_q, w_mem_kv, w_mem_o,
              norm_ffn, w_ffn_in, w_ffn_out, norm_final):
    b, s, _ = x.shape
    split_points = list(np.cumsum(IN_SPLITS)[:-1])
    for l in range(DEPTH):
        h = rms_norm(x, norm_mix[l])
        proj = h @ w_in[l]
        q_a, k_a, v_a, u_b, gate_b, gate_c, g_a, g_b = jnp.split(proj, split_points, axis=-1)

        def heads(t):
            return t.reshape(b, s, SB_HEADS, SB_HEAD_DIM).transpose(0, 2, 1, 3)

        o_a = stick_breaking_attention(heads(q_a), heads(k_a), heads(v_a))
        o_a = o_a.transpose(0, 2, 1, 3).reshape(b, s, SB_WIDTH)
        y_b = gate_b * causal_depthwise_conv(gate_c * u_b, conv_w[l])

        branch_a = o_a @ w_branch_a[l]
        branch_b = y_b @ w_branch_b[l]
        merged = jax.nn.sigmoid(g_a) * branch_a + jax.nn.sigmoid(g_b) * branch_b
        x = x + merged @ w_mix_out[l]

        x = x + memory_cross_attention(rms_norm(x, norm_mem_q[l]), rms_norm(mem, norm_mem_kv[l]),
                                       w_mem_q[l], w_mem_kv[l], w_mem_o[l])

        hf = rms_norm(x, norm_ffn[l])
        gate, up = jnp.split(hf @ w_ffn_in[l], 2, axis=-1)
        x = x + (jax.nn.silu(gate) * up) @ w_ffn_out[l]
    return rms_norm(x, norm_final)
```

```python
import functools

import jax
import jax.numpy as jnp
from jax import lax
from jax.experimental import pallas as pl
from jax.experimental.pallas import tpu as pltpu

EPS = 1e-6
SB_HEADS = 8
SB_HEAD_DIM = 64
CONV_K = 3
MEM_HEADS = 4

LANES = 128
SUBLANES = 8
TOKEN_TILE = 512
SB_TILE = 256
MIB = 1024 * 1024

BF16 = jnp.bfloat16
F32 = jnp.float32


def _dot(a, b):
    return jnp.dot(a, b, preferred_element_type=F32)


def _dot_nt(a, b):
    return lax.dot_general(a, b, (((1,), (1,)), ((), ())), preferred_element_type=F32)


def _rms(x, g):
    var = jnp.mean(x * x, axis=-1, keepdims=True)
    return x * lax.rsqrt(var + EPS) * g


def _const_spec(shape):
    zeros = (0,) * len(shape)
    return pl.BlockSpec(shape, lambda *_: zeros, pipeline_mode=pl.Buffered(1))


def _mix_in_kernel(x_ref, g_ref, wq_ref, wkt_ref, wv_ref, wconv_ref, wga_ref, wgb_ref,
                   cw_ref, wbb_ref,
                   q_ref, kt_ref, v_ref, sa_ref, mb_ref, cu_buf):
    tm = x_ref.shape[1]
    cw = wbb_ref.shape[0]
    halo = SUBLANES

    @pl.when(pl.program_id(1) == 0)
    def _():
        cu_buf[0:halo, :] = jnp.zeros((halo, cw), F32)

    h = _rms(x_ref[0], g_ref[...]).astype(BF16)

    q_ref[0] = (_dot(h, wq_ref[...]) * (SB_HEAD_DIM ** -0.5)).astype(BF16)
    kt = _dot_nt(wkt_ref[...], h).astype(BF16)
    for p in range(kt_ref.shape[1]):
        for j in range(kt_ref.shape[2]):
            kt_ref[0, p, j] = kt[p * LANES:(p + 1) * LANES, j * SB_TILE:(j + 1) * SB_TILE]
    v_ref[0] = _dot(h, wv_ref[...]).astype(BF16)

    ugc = _dot(h, wconv_ref[...])
    cu = ugc[:, 2 * cw:3 * cw] * ugc[:, 0:cw]
    gate_b = ugc[:, cw:2 * cw]
    cu_buf[halo:halo + tm, :] = cu
    conv = cw_ref[2:3, :] * cu
    for i in range(CONV_K - 1):
        shift = CONV_K - 1 - i
        conv = conv + cw_ref[i:i + 1, :] * cu_buf[halo - shift:halo - shift + tm, :]
    cu_buf[0:halo, :] = cu[tm - halo:tm, :]
    yb = (gate_b * conv).astype(BF16)
    branch_b = _dot(yb, wbb_ref[...])
    mb_ref[0] = (jax.nn.sigmoid(_dot(h, wgb_ref[...])) * branch_b).astype(BF16)
    sa_ref[0] = jax.nn.sigmoid(_dot(h, wga_ref[...])).astype(BF16)


def _mix_in(x, g, wq, wkt, wv, wconv, wga, wgb, cw, wbb):
    b, s, d = x.shape
    tm = TOKEN_TILE
    sbw = wq.shape[1]
    cwid = wbb.shape[0]
    pairs = sbw // LANES
    out_shape = (
        jax.ShapeDtypeStruct((b, s, sbw), BF16),
        jax.ShapeDtypeStruct((b, pairs, s // SB_TILE, LANES, SB_TILE), BF16),
        jax.ShapeDtypeStruct((b, s, sbw), BF16),
        jax.ShapeDtypeStruct((b, s, d), BF16),
        jax.ShapeDtypeStruct((b, s, d), BF16),
    )
    row = lambda bi, i: (bi, i, 0)
    return pl.pallas_call(
        _mix_in_kernel,
        grid=(b, s // tm),
        in_specs=[
            pl.BlockSpec((1, tm, d), row),
            _const_spec(g.shape), _const_spec(wq.shape), _const_spec(wkt.shape),
            _const_spec(wv.shape), _const_spec(wconv.shape), _const_spec(wga.shape),
            _const_spec(wgb.shape), _const_spec(cw.shape), _const_spec(wbb.shape),
        ],
        out_specs=(
            pl.BlockSpec((1, tm, sbw), row),
            pl.BlockSpec((1, pairs, tm // SB_TILE, LANES, SB_TILE), lambda bi, i: (bi, 0, i, 0, 0)),
            pl.BlockSpec((1, tm, sbw), row),
            pl.BlockSpec((1, tm, d), row),
            pl.BlockSpec((1, tm, d), row),
        ),
        out_shape=out_shape,
        scratch_shapes=[pltpu.VMEM((tm + SUBLANES, cwid), F32)],
        compiler_params=pltpu.CompilerParams(
            dimension_semantics=("arbitrary", "arbitrary"),
            vmem_limit_bytes=48 * MIB),
        name="mix_in",
    )(x, g, wq, wkt, wv, wconv, wga, wgb, cw, wbb)


def _sb_attn_kernel(q_ref, kt_ref, v_ref, tri_ref, o_ref, acc_ref, c_ref):
    tq = q_ref.shape[1]
    i = pl.program_id(2)
    q = q_ref[0]
    lane = lax.broadcasted_iota(jnp.int32, q.shape, 1)
    first = lane < SB_HEAD_DIM
    qh = (jnp.where(first, q, 0).astype(BF16), jnp.where(first, 0, q).astype(BF16))
    tri = tri_ref[...]

    row = lax.broadcasted_iota(jnp.int32, (tq, SB_TILE), 0)
    col = lax.broadcasted_iota(jnp.int32, (tq, SB_TILE), 1)
    causal = col < row

    def tile(hd, kt, v, masked):
        z = _dot(qh[hd], kt)
        p = jnp.maximum(z, 0.0) + jnp.log(1.0 + jnp.exp(-jnp.abs(z)))
        if masked:
            p = jnp.where(causal, p, 0.0)
        later = _dot(p.astype(BF16), tri)
        w = jnp.exp(z - p - later + c_ref[hd])
        if masked:
            w = jnp.where(causal, w, 0.0)
        acc_ref[hd] += _dot(w.astype(BF16), v)
        c_ref[hd] -= jnp.sum(p, axis=-1, keepdims=True)

    acc_ref[...] = jnp.zeros(acc_ref.shape, F32)
    c_ref[...] = jnp.zeros(c_ref.shape, F32)

    kt_d = kt_ref[0, 0, i]
    v_d = v_ref[0, pl.ds(pl.multiple_of(i * SB_TILE, SB_TILE), SB_TILE), :]
    for hd in range(2):
        tile(hd, kt_d, v_d, True)

    def body(n, carry):
        j = i - 1 - n
        kt = kt_ref[0, 0, j]
        v = v_ref[0, pl.ds(pl.multiple_of(j * SB_TILE, SB_TILE), SB_TILE), :]
        for hd in range(2):
            tile(hd, kt, v, False)
        return carry

    lax.fori_loop(0, i, body, 0)
    o_ref[0] = jnp.where(first, acc_ref[0], acc_ref[1]).astype(BF16)


def _sb_attn(q, kt, v, tri):
    b, s, sbw = q.shape
    pairs = sbw // LANES
    tq = SB_TILE
    return pl.pallas_call(
        _sb_attn_kernel,
        grid=(b, pairs, s // tq),
        in_specs=[
            pl.BlockSpec((1, tq, LANES), lambda bi, p, i: (bi, i, p)),
            pl.BlockSpec((1, 1, s // SB_TILE, LANES, SB_TILE), lambda bi, p, i: (bi, p, 0, 0, 0)),
            pl.BlockSpec((1, s, LANES), lambda bi, p, i: (bi, 0, p)),
            _const_spec(tri.shape),
        ],
        out_specs=pl.BlockSpec((1, tq, LANES), lambda bi, p, i: (bi, i, p)),
        out_shape=jax.ShapeDtypeStruct((b, s, sbw), BF16),
        scratch_shapes=[pltpu.VMEM((2, tq, LANES), F32), pltpu.VMEM((2, tq, 1), F32)],
        compiler_params=pltpu.CompilerParams(
            dimension_semantics=("parallel", "parallel", "parallel"),
            vmem_limit_bytes=32 * MIB),
        name="sb_attn",
    )(q, kt, v, tri)


def _mem_kv_kernel(mem_ref, g_ref, wkt_ref, wv_ref, kt_ref, v_ref):
    hm = _rms(mem_ref[0], g_ref[...]).astype(BF16)
    kt = _dot_nt(wkt_ref[...], hm).astype(BF16)
    dh = kt_ref.shape[2]
    for hd in range(kt_ref.shape[1]):
        kt_ref[0, hd] = kt[hd * dh:(hd + 1) * dh, :]
    v_ref[0] = _dot(hm, wv_ref[...]).astype(BF16)


def _mem_kv(mem, g, wkt, wv):
    b, m, d = mem.shape
    dh = d // MEM_HEADS
    return pl.pallas_call(
        _mem_kv_kernel,
        grid=(b,),
        in_specs=[pl.BlockSpec((1, m, d), lambda bi: (bi, 0, 0)),
                  _const_spec(g.shape), _const_spec(wkt.shape), _const_spec(wv.shape)],
        out_specs=(pl.BlockSpec((1, MEM_HEADS, dh, m), lambda bi: (bi, 0, 0, 0)),
                   pl.BlockSpec((1, m, d), lambda bi: (bi, 0, 0))),
        out_shape=(jax.ShapeDtypeStruct((b, MEM_HEADS, dh, m), BF16),
                   jax.ShapeDtypeStruct((b, m, d), BF16)),
        compiler_params=pltpu.CompilerParams(
            dimension_semantics=("parallel",), vmem_limit_bytes=32 * MIB),
        name="mem_kv",
    )(mem, g, wkt, wv)


def _mix_out_kernel(x_ref, oa_ref, sa_ref, mb_ref, wba_ref, wmo_ref, gq_ref, wmq_ref,
                    kmt_ref, vm_ref, wmemo_ref, o_ref):
    branch_a = _dot(oa_ref[0], wba_ref[...])
    merged = (sa_ref[0].astype(F32) * branch_a + mb_ref[0].astype(F32)).astype(BF16)
    x1 = x_ref[0] + _dot(merged, wmo_ref[...])

    dh = kmt_ref.shape[2]
    hq = _rms(x1, gq_ref[...]).astype(BF16)
    qm = (_dot(hq, wmq_ref[...]) * (dh ** -0.5)).astype(BF16)
    heads = []
    for hd in range(MEM_HEADS):
        sc = _dot(qm[:, hd * dh:(hd + 1) * dh], kmt_ref[0, hd])
        e = jnp.exp(sc - jnp.max(sc, axis=-1, keepdims=True))
        denom = jnp.sum(e, axis=-1, keepdims=True)
        o = _dot(e.astype(BF16), vm_ref[0, :, hd * dh:(hd + 1) * dh]) / denom
        heads.append(o.astype(BF16))
    o_ref[0] = x1 + _dot(jnp.concatenate(heads, axis=1), wmemo_ref[...])


def _mix_out(x, oa, sa, mb, wba, wmo, gq, wmq, kmt, vm, wmemo):
    b, s, d = x.shape
    tm = TOKEN_TILE
    row = lambda bi, i: (bi, i, 0)
    return pl.pallas_call(
        _mix_out_kernel,
        grid=(b, s // tm),
        in_specs=[
            pl.BlockSpec((1, tm, d), row),
            pl.BlockSpec((1, tm, oa.shape[2]), row),
            pl.BlockSpec((1, tm, d), row),
            pl.BlockSpec((1, tm, d), row),
            _const_spec(wba.shape), _const_spec(wmo.shape), _const_spec(gq.shape),
            _const_spec(wmq.shape),
            pl.BlockSpec((1,) + kmt.shape[1:], lambda bi, i: (bi, 0, 0, 0)),
            pl.BlockSpec((1,) + vm.shape[1:], lambda bi, i: (bi, 0, 0)),
            _const_spec(wmemo.shape),
        ],
        out_specs=pl.BlockSpec((1, tm, d), row),
        out_shape=jax.ShapeDtypeStruct((b, s, d), F32),
        compiler_params=pltpu.CompilerParams(
            dimension_semantics=("parallel", "parallel"), vmem_limit_bytes=48 * MIB),
        name="mix_out",
    )(x, oa, sa, mb, wba, wmo, gq, wmq, kmt, vm, wmemo)


def _ffn_kernel(x_ref, gf_ref, wg_ref, wu_ref, wo_ref, gfin_ref, o_ref, *, chunks, final_norm):
    x = x_ref[0]
    h = _rms(x, gf_ref[...]).astype(BF16)
    step = wg_ref.shape[1] // chunks
    y = x
    for c in range(chunks):
        sl = slice(c * step, (c + 1) * step)
        gate = _dot(h, wg_ref[:, sl])
        up = _dot(h, wu_ref[:, sl])
        act = (gate * jax.nn.sigmoid(gate) * up).astype(BF16)
        y = y + _dot(act, wo_ref[sl, :])
    o_ref[0] = _rms(y, gfin_ref[...]) if final_norm else y


def _ffn(x, gf, wg, wu, wo, gfin, final_norm):
    b, s, d = x.shape
    tm = TOKEN_TILE
    row = lambda bi, i: (bi, i, 0)
    hidden = wg.shape[1]
    chunks = 2
    assert hidden % (chunks * LANES) == 0
    return pl.pallas_call(
        functools.partial(_ffn_kernel, chunks=chunks, final_norm=final_norm),
        grid=(b, s // tm),
        in_specs=[pl.BlockSpec((1, tm, d), row), _const_spec(gf.shape), _const_spec(wg.shape),
                  _const_spec(wu.shape), _const_spec(wo.shape), _const_spec(gfin.shape)],
        out_specs=pl.BlockSpec((1, tm, d), row),
        out_shape=jax.ShapeDtypeStruct((b, s, d), F32),
        compiler_params=pltpu.CompilerParams(
            dimension_semantics=("parallel", "parallel"), vmem_limit_bytes=56 * MIB),
        name="ffn",
    )(x, gf, wg, wu, wo, gfin)


def kernel(x, mem, norm_mix, w_in, conv_w, w_branch_a, w_branch_b, w_mix_out, norm_mem_q,
           norm_mem_kv, w_mem_q, w_mem_kv, w_mem_o, norm_ffn, w_ffn_in, w_ffn_out, norm_final):
    depth = w_in.shape[0]
    d = x.shape[-1]
    sbw = SB_HEADS * SB_HEAD_DIM
    cwid = conv_w.shape[-1]
    hidden = w_ffn_out.shape[1]
    assert x.shape[1] % TOKEN_TILE == 0 and TOKEN_TILE % SB_TILE == 0 and sbw % LANES == 0

    idx = jnp.arange(SB_TILE)
    tri = (idx[:, None] > idx[None, :]).astype(BF16)

    for l in range(depth):
        wi = w_in[l].astype(BF16)
        o0 = 3 * sbw
        o1 = o0 + 3 * cwid
        q, kt, v, sa, mb = _mix_in(
            x, norm_mix[l][None, :],
            wi[:, 0:sbw], wi[:, sbw:2 * sbw].T, wi[:, 2 * sbw:o0], wi[:, o0:o1],
            wi[:, o1:o1 + d], wi[:, o1 + d:o1 + 2 * d],
            conv_w[l], w_branch_b[l].astype(BF16))
        oa = _sb_attn(q, kt, v, tri)
        wkv = w_mem_kv[l].astype(BF16)
        kmt, vm = _mem_kv(mem, norm_mem_kv[l][None, :], wkv[:, :d].T, wkv[:, d:])
        x = _mix_out(x, oa, sa, mb, w_branch_a[l].astype(BF16), w_mix_out[l].astype(BF16),
                     norm_mem_q[l][None, :], w_mem_q[l].astype(BF16), kmt, vm,
                     w_mem_o[l].astype(BF16))
        wf = w_ffn_in[l].astype(BF16)
        x = _ffn(x, norm_ffn[l][None, :], wf[:, :hidden], wf[:, hidden:],
                 w_ffn_out[l].astype(BF16), norm_final[None, :], final_norm=(l == depth - 1))
    return x
```

```python
import functools
import math

import jax
import jax.numpy as jnp
from jax import lax
from jax.experimental import pallas as pl
from jax.experimental.pallas import tpu as pltpu

EPS = 1e-6
SB_HEADS = 8
SB_HEAD_DIM = 64
CONV_K = 3
MEM_HEADS = 4

LANES = 128
SUBLANES = 8
TOKEN_TILE = 512
SB_TILE = 256
MIB = 1024 * 1024

LOG2E = math.log2(math.e)
MASKED = -1e30
DEAD_LOG2 = -160.0

BF16 = jnp.bfloat16
F32 = jnp.float32


def _dot(a, b):
    return jnp.dot(a, b, preferred_element_type=F32)


def _dot_nt(a, b):
    return lax.dot_general(a, b, (((1,), (1,)), ((), ())), preferred_element_type=F32)


def _rms(x, g):
    var = jnp.mean(x * x, axis=-1, keepdims=True)
    return x * lax.rsqrt(var + EPS) * g


def _const_spec(shape):
    zeros = (0,) * len(shape)
    return pl.BlockSpec(shape, lambda *_: zeros, pipeline_mode=pl.Buffered(1))


def _mix_in_kernel(x_ref, g_ref, wqt_ref, wk_ref, wvt_ref, wconv_ref, wga_ref, wgb_ref,
                   cw_ref, wbb_ref,
                   qt_ref, k_ref, vt_ref, sa_ref, mb_ref, cu_buf):
    tm = x_ref.shape[1]
    cw = wbb_ref.shape[0]
    halo = SUBLANES

    @pl.when(pl.program_id(1) == 0)
    def _():
        cu_buf[0:halo, :] = jnp.zeros((halo, cw), F32)

    h = _rms(x_ref[0], g_ref[...]).astype(BF16)

    qt = (_dot_nt(wqt_ref[...], h) * (LOG2E * SB_HEAD_DIM ** -0.5)).astype(BF16)
    vt = _dot_nt(wvt_ref[...], h).astype(BF16)
    for p in range(qt_ref.shape[1]):
        for j in range(qt_ref.shape[2]):
            rows, cols = slice(p * LANES, (p + 1) * LANES), slice(j * SB_TILE, (j + 1) * SB_TILE)
            qt_ref[0, p, j] = qt[rows, cols]
            vt_ref[0, p, j] = vt[rows, cols]
    k_ref[0] = _dot(h, wk_ref[...]).astype(BF16)

    ugc = _dot(h, wconv_ref[...])
    cu = ugc[:, 2 * cw:3 * cw] * ugc[:, 0:cw]
    gate_b = ugc[:, cw:2 * cw]
    cu_buf[halo:halo + tm, :] = cu
    conv = cw_ref[2:3, :] * cu
    for i in range(CONV_K - 1):
        shift = CONV_K - 1 - i
        conv = conv + cw_ref[i:i + 1, :] * cu_buf[halo - shift:halo - shift + tm, :]
    cu_buf[0:halo, :] = cu[tm - halo:tm, :]
    yb = (gate_b * conv).astype(BF16)
    branch_b = _dot(yb, wbb_ref[...])
    mb_ref[0] = (jax.nn.sigmoid(_dot(h, wgb_ref[...])) * branch_b).astype(BF16)
    sa_ref[0] = jax.nn.sigmoid(_dot(h, wga_ref[...])).astype(BF16)


def _mix_in(x, g, wqt, wk, wvt, wconv, wga, wgb, cw, wbb):
    b, s, d = x.shape
    tm = TOKEN_TILE
    sbw = wk.shape[1]
    cwid = wbb.shape[0]
    pairs = sbw // LANES
    tiled = jax.ShapeDtypeStruct((b, pairs, s // SB_TILE, LANES, SB_TILE), BF16)
    out_shape = (
        tiled,
        jax.ShapeDtypeStruct((b, s, sbw), BF16),
        tiled,
        jax.ShapeDtypeStruct((b, s, d), BF16),
        jax.ShapeDtypeStruct((b, s, d), BF16),
    )
    row = lambda bi, i: (bi, i, 0)
    tiled_spec = pl.BlockSpec((1, pairs, tm // SB_TILE, LANES, SB_TILE),
                              lambda bi, i: (bi, 0, i, 0, 0))
    return pl.pallas_call(
        _mix_in_kernel,
        grid=(b, s // tm),
        in_specs=[
            pl.BlockSpec((1, tm, d), row),
            _const_spec(g.shape), _const_spec(wqt.shape), _const_spec(wk.shape),
            _const_spec(wvt.shape), _const_spec(wconv.shape), _const_spec(wga.shape),
            _const_spec(wgb.shape), _const_spec(cw.shape), _const_spec(wbb.shape),
        ],
        out_specs=(
            tiled_spec,
            pl.BlockSpec((1, tm, sbw), row),
            tiled_spec,
            pl.BlockSpec((1, tm, d), row),
            pl.BlockSpec((1, tm, d), row),
        ),
        out_shape=out_shape,
        scratch_shapes=[pltpu.VMEM((tm + SUBLANES, cwid), F32)],
        compiler_params=pltpu.CompilerParams(
            dimension_semantics=("arbitrary", "arbitrary"),
            vmem_limit_bytes=48 * MIB),
        name="mix_in",
    )(x, g, wqt, wk, wvt, wconv, wga, wgb, cw, wbb)


SB_STAGES = 5
SB_SLOTS = 4


def _sb_attn_kernel(qt_ref, k_ref, vt_ref, tri_ref, o_ref,
                    qm_ref, c_ref, acc_ref, z_buf, p_buf, cum_buf, w_buf):
    nq, tq = qt_ref.shape[2], qt_ref.shape[4]
    tk = SB_TILE
    hd = SB_HEAD_DIM
    dummy = nq

    def prep(i, carry):
        qt = qt_ref[0, 0, i]
        zero = jnp.zeros((hd, tq), BF16)
        qm_ref[0, i, 0:hd, :] = qt[0:hd]
        qm_ref[0, i, hd:2 * hd, :] = zero
        qm_ref[1, i, 0:hd, :] = zero
        qm_ref[1, i, hd:2 * hd, :] = qt[hd:2 * hd]
        return carry

    lax.fori_loop(0, nq, prep, 0)
    qm_ref[:, dummy] = jnp.zeros((2, 2 * hd, tq), BF16)
    c_ref[...] = jnp.zeros(c_ref.shape, F32)
    acc_ref[...] = jnp.zeros(acc_ref.shape, F32)
    z_buf[...] = jnp.zeros(z_buf.shape, F32)
    p_buf[...] = jnp.zeros(p_buf.shape, BF16)
    cum_buf[...] = jnp.zeros(cum_buf.shape, F32)
    w_buf[...] = jnp.zeros(w_buf.shape, BF16)

    key = lax.broadcasted_iota(jnp.int32, (tk, tq), 0)
    qry = lax.broadcasted_iota(jnp.int32, (tk, tq), 1)
    causal = key < qry

    def stage_ma(i, j, h, slot):
        kt = k_ref[0, pl.ds(pl.multiple_of(j * tk, tk), tk), :]
        z_buf[slot] = _dot(kt, qm_ref[h, i])

    def stage_ea(slot, masked):
        z = z_buf[slot]
        if masked:
            z = jnp.where(causal, z, MASKED)
            z_buf[slot] = z
        p = jnp.maximum(z, 0.0) + jnp.log(1.0 + jnp.exp2(-jnp.abs(z))) * LOG2E
        p_buf[slot] = p.astype(BF16)

    def stage_mb(slot):
        cum_buf[slot] = _dot(tri_ref[...], p_buf[slot])

    def stage_eb(i, h, slot):
        cum = cum_buf[slot]
        c = c_ref[h, i, 0:1, :]
        w_buf[slot] = jnp.exp2(z_buf[slot] - cum + c).astype(BF16)
        c_ref[h, i] = jnp.broadcast_to(c - cum[0:1, :], (SUBLANES, tq))

    def stage_mc(i, j, h, slot):
        vt = vt_ref[0, 0, j, h * hd:(h + 1) * hd, :]
        acc_ref[h, i] += _dot(vt, w_buf[slot])

    def run_distance(d, masked):
        n_items = 2 * (nq - d)

        def item(m):
            valid = jnp.logical_and(m >= 0, m < n_items)
            t = m // 2
            return jnp.where(valid, d + t, dummy), jnp.where(valid, t, 0)

        def body(u, carry):
            for r in range(SB_SLOTS):
                s = SB_SLOTS * u + r
                i4, j4 = item(s - 4)
                stage_mc(i4, j4, (r - 4) % 2, (r - 4) % SB_SLOTS)
                i3, _ = item(s - 3)
                stage_eb(i3, (r - 3) % 2, (r - 3) % SB_SLOTS)
                stage_mb((r - 2) % SB_SLOTS)
                stage_ea((r - 1) % SB_SLOTS, masked)
                i0, j0 = item(s)
                stage_ma(i0, j0, r % 2, r)
            return carry

        steps = n_items + SB_STAGES - 1
        lax.fori_loop(0, (steps + SB_SLOTS - 1) // SB_SLOTS, body, 0)

    def alive_after(d):
        cc = jnp.maximum(c_ref[0, 0:nq], c_ref[1, 0:nq])
        blk = lax.broadcasted_iota(jnp.int32, cc.shape, 0)
        top = jnp.max(jnp.where(blk > d, cc, MASKED), axis=0)
        return (jnp.max(top) > DEAD_LOG2).astype(jnp.int32)

    run_distance(0, True)

    def more(state):
        d, alive = state
        return jnp.logical_and(d < nq, alive > 0)

    def step(state):
        d, _ = state
        run_distance(d, False)
        return d + 1, alive_after(d)

    lax.while_loop(more, step, (jnp.int32(1), alive_after(0)))

    def emit(i, carry):
        both = jnp.concatenate([acc_ref[0, i], acc_ref[1, i]], axis=0)
        o_ref[0, pl.ds(pl.multiple_of(i * tq, tq), tq), :] = both.T.astype(BF16)
        return carry

    lax.fori_loop(0, nq, emit, 0)


def _sb_attn(qt, k, vt, tri):
    b, pairs, nq, _, tq = qt.shape
    s, sbw = k.shape[1], k.shape[2]
    tk = SB_TILE
    return pl.pallas_call(
        _sb_attn_kernel,
        grid=(b, pairs),
        in_specs=[
            pl.BlockSpec((1, 1, nq, LANES, tq), lambda bi, p: (bi, p, 0, 0, 0)),
            pl.BlockSpec((1, s, LANES), lambda bi, p: (bi, 0, p)),
            pl.BlockSpec((1, 1, s // tk, LANES, tk), lambda bi, p: (bi, p, 0, 0, 0)),
            _const_spec(tri.shape),
        ],
        out_specs=pl.BlockSpec((1, s, LANES), lambda bi, p: (bi, 0, p)),
        out_shape=jax.ShapeDtypeStruct((b, s, sbw), BF16),
        scratch_shapes=[
            pltpu.VMEM((2, nq + 1, LANES, tq), BF16),
            pltpu.VMEM((2, nq + 1, SUBLANES, tq), F32),
            pltpu.VMEM((2, nq + 1, SB_HEAD_DIM, tq), F32),
            pltpu.VMEM((SB_SLOTS, tk, tq), F32),
            pltpu.VMEM((SB_SLOTS, tk, tq), BF16),
            pltpu.VMEM((SB_SLOTS, tk, tq), F32),
            pltpu.VMEM((SB_SLOTS, tk, tq), BF16),
        ],
        compiler_params=pltpu.CompilerParams(
            dimension_semantics=("parallel", "parallel"),
            vmem_limit_bytes=32 * MIB),
        name="sb_attn",
    )(qt, k, vt, tri)


def _mem_kv_kernel(mem_ref, g_ref, wkt_ref, wv_ref, kt_ref, v_ref):
    hm = _rms(mem_ref[0], g_ref[...]).astype(BF16)
    kt = _dot_nt(wkt_ref[...], hm).astype(BF16)
    dh = kt_ref.shape[2]
    for hd in range(kt_ref.shape[1]):
        kt_ref[0, hd] = kt[hd * dh:(hd + 1) * dh, :]
    v_ref[0] = _dot(hm, wv_ref[...]).astype(BF16)


def _mem_kv(mem, g, wkt, wv):
    b, m, d = mem.shape
    dh = d // MEM_HEADS
    return pl.pallas_call(
        _mem_kv_kernel,
        grid=(b,),
        in_specs=[pl.BlockSpec((1, m, d), lambda bi: (bi, 0, 0)),
                  _const_spec(g.shape), _const_spec(wkt.shape), _const_spec(wv.shape)],
        out_specs=(pl.BlockSpec((1, MEM_HEADS, dh, m), lambda bi: (bi, 0, 0, 0)),
                   pl.BlockSpec((1, m, d), lambda bi: (bi, 0, 0))),
        out_shape=(jax.ShapeDtypeStruct((b, MEM_HEADS, dh, m), BF16),
                   jax.ShapeDtypeStruct((b, m, d), BF16)),
        compiler_params=pltpu.CompilerParams(
            dimension_semantics=("parallel",), vmem_limit_bytes=32 * MIB),
        name="mem_kv",
    )(mem, g, wkt, wv)


def _mix_out_kernel(x_ref, oa_ref, sa_ref, mb_ref, wba_ref, wmo_ref, gq_ref, wmq_ref,
                    kmt_ref, vm_ref, wmemo_ref, o_ref):
    branch_a = _dot(oa_ref[0], wba_ref[...])
    merged = (sa_ref[0].astype(F32) * branch_a + mb_ref[0].astype(F32)).astype(BF16)
    x1 = x_ref[0] + _dot(merged, wmo_ref[...])

    dh = kmt_ref.shape[2]
    hq = _rms(x1, gq_ref[...]).astype(BF16)
    qm = (_dot(hq, wmq_ref[...]) * (dh ** -0.5)).astype(BF16)
    heads = []
    for hd in range(MEM_HEADS):
        sc = _dot(qm[:, hd * dh:(hd + 1) * dh], kmt_ref[0, hd])
        e = jnp.exp(sc - jnp.max(sc, axis=-1, keepdims=True))
        denom = jnp.sum(e, axis=-1, keepdims=True)
        o = _dot(e.astype(BF16), vm_ref[0, :, hd * dh:(hd + 1) * dh]) / denom
        heads.append(o.astype(BF16))
    o_ref[0] = x1 + _dot(jnp.concatenate(heads, axis=1), wmemo_ref[...])


def _mix_out(x, oa, sa, mb, wba, wmo, gq, wmq, kmt, vm, wmemo):
    b, s, d = x.shape
    tm = TOKEN_TILE
    row = lambda bi, i: (bi, i, 0)
    return pl.pallas_call(
        _mix_out_kernel,
        grid=(b, s // tm),
        in_specs=[
            pl.BlockSpec((1, tm, d), row),
            pl.BlockSpec((1, tm, oa.shape[2]), row),
            pl.BlockSpec((1, tm, d), row),
            pl.BlockSpec((1, tm, d), row),
            _const_spec(wba.shape), _const_spec(wmo.shape), _const_spec(gq.shape),
            _const_spec(wmq.shape),
            pl.BlockSpec((1,) + kmt.shape[1:], lambda bi, i: (bi, 0, 0, 0)),
            pl.BlockSpec((1,) + vm.shape[1:], lambda bi, i: (bi, 0, 0)),
            _const_spec(wmemo.shape),
        ],
        out_specs=pl.BlockSpec((1, tm, d), row),
        out_shape=jax.ShapeDtypeStruct((b, s, d), F32),
        compiler_params=pltpu.CompilerParams(
            dimension_semantics=("parallel", "parallel"), vmem_limit_bytes=48 * MIB),
        name="mix_out",
    )(x, oa, sa, mb, wba, wmo, gq, wmq, kmt, vm, wmemo)


def _ffn_kernel(x_ref, gf_ref, wg_ref, wu_ref, wo_ref, gfin_ref, o_ref, *, chunks, final_norm):
    x = x_ref[0]
    h = _rms(x, gf_ref[...]).astype(BF16)
    step = wg_ref.shape[1] // chunks
    y = x
    for c in range(chunks):
        sl = slice(c * step, (c + 1) * step)
        gate = _dot(h, wg_ref[:, sl])
        up = _dot(h, wu_ref[:, sl])
        act = (gate * jax.nn.sigmoid(gate) * up).astype(BF16)
        y = y + _dot(act, wo_ref[sl, :])
    o_ref[0] = _rms(y, gfin_ref[...]) if final_norm else y


def _ffn(x, gf, wg, wu, wo, gfin, final_norm):
    b, s, d = x.shape
    tm = TOKEN_TILE
    row = lambda bi, i: (bi, i, 0)
    hidden = wg.shape[1]
    chunks = 2
    assert hidden % (chunks * LANES) == 0
    return pl.pallas_call(
        functools.partial(_ffn_kernel, chunks=chunks, final_norm=final_norm),
        grid=(b, s // tm),
        in_specs=[pl.BlockSpec((1, tm, d), row), _const_spec(gf.shape), _const_spec(wg.shape),
                  _const_spec(wu.shape), _const_spec(wo.shape), _const_spec(gfin.shape)],
        out_specs=pl.BlockSpec((1, tm, d), row),
        out_shape=jax.ShapeDtypeStruct((b, s, d), F32),
        compiler_params=pltpu.CompilerParams(
            dimension_semantics=("parallel", "parallel"), vmem_limit_bytes=56 * MIB),
        name="ffn",
    )(x, gf, wg, wu, wo, gfin)


def kernel(x, mem, norm_mix, w_in, conv_w, w_branch_a, w_branch_b, w_mix_out, norm_mem_q,
           norm_mem_kv, w_mem_q, w_mem_kv, w_mem_o, norm_ffn, w_ffn_in, w_ffn_out, norm_final):
    depth = w_in.shape[0]
    d = x.shape[-1]
    sbw = SB_HEADS * SB_HEAD_DIM
    cwid = conv_w.shape[-1]
    hidden = w_ffn_out.shape[1]
    assert x.shape[1] % TOKEN_TILE == 0 and TOKEN_TILE % SB_TILE == 0
    assert LANES == 2 * SB_HEAD_DIM and sbw % LANES == 0

    idx = jnp.arange(SB_TILE)
    tri = (idx[None, :] >= idx[:, None]).astype(BF16)

    for l in range(depth):
        wi = w_in[l].astype(BF16)
        o0 = 3 * sbw
        o1 = o0 + 3 * cwid
        qt, k, vt, sa, mb = _mix_in(
            x, norm_mix[l][None, :],
            wi[:, 0:sbw].T, wi[:, sbw:2 * sbw], wi[:, 2 * sbw:o0].T, wi[:, o0:o1],
            wi[:, o1:o1 + d], wi[:, o1 + d:o1 + 2 * d],
            conv_w[l], w_branch_b[l].astype(BF16))
        oa = _sb_attn(qt, k, vt, tri)
        wkv = w_mem_kv[l].astype(BF16)
        kmt, vm = _mem_kv(mem, norm_mem_kv[l][None, :], wkv[:, :d].T, wkv[:, d:])
        x = _mix_out(x, oa, sa, mb, w_branch_a[l].astype(BF16), w_mix_out[l].astype(BF16),
                     norm_mem_q[l][None, :], w_mem_q[l].astype(BF16), kmt, vm,
                     w_mem_o[l].astype(BF16))
        wf = w_ffn_in[l].astype(BF16)
        x = _ffn(x, norm_ffn[l][None, :], wf[:, :hidden], wf[:, hidden:],
                 w_ffn_out[l].astype(BF16), norm_final[None, :], final_norm=(l == depth - 1))
    return x
```

```python
import functools
import math

import jax
import jax.numpy as jnp
from jax import lax
from jax.experimental import pallas as pl
from jax.experimental.pallas import tpu as pltpu

EPS = 1e-6
SB_HEADS = 8
SB_HEAD_DIM = 64
CONV_K = 3
MEM_HEADS = 4

LANES = 128
SUBLANES = 8
MXU_DIM = 256
TOKEN_TILE = 512
SB_TILE = 256
MIB = 1024 * 1024

LOG2E = math.log2(math.e)
MASKED = -1e30
DEAD_LOG2 = -160.0

BF16 = jnp.bfloat16
F32 = jnp.float32


def _dot(a, b):
    return jnp.dot(a, b, preferred_element_type=F32)


def _dot_nt(a, b):
    return lax.dot_general(a, b, (((1,), (1,)), ((), ())), preferred_element_type=F32)


def _dot_tn(a, b):
    return lax.dot_general(a, b, (((0,), (1,)), ((), ())), preferred_element_type=F32)


def _rms(x, g):
    var = jnp.mean(x * x, axis=-1, keepdims=True)
    return x * lax.rsqrt(var + EPS) * g


def _const_spec(shape):
    zeros = (0,) * len(shape)
    return pl.BlockSpec(shape, lambda *_: zeros, pipeline_mode=pl.Buffered(1))


def _mix_in_kernel(x_ref, g_ref, win_ref, cw_ref, wbb_ref,
                   qt_ref, k_ref, vt_ref, sa_ref, mb_ref, cu_buf):
    tm = x_ref.shape[1]
    d = x_ref.shape[2]
    cw = wbb_ref.shape[0]
    sbw = k_ref.shape[2]
    halo = SUBLANES
    o_conv = 3 * sbw
    o_ga = o_conv + 3 * cw
    o_gb = o_ga + d

    @pl.when(pl.program_id(1) == 0)
    def _():
        cu_buf[0:halo, :] = jnp.zeros((halo, cw), F32)

    h = _rms(x_ref[0], g_ref[...]).astype(BF16)

    qt = (_dot_tn(win_ref[:, 0:sbw], h) * (LOG2E * SB_HEAD_DIM ** -0.5)).astype(BF16)
    vt = _dot_tn(win_ref[:, 2 * sbw:3 * sbw], h).astype(BF16)
    for p in range(qt_ref.shape[1]):
        for j in range(qt_ref.shape[2]):
            rows, cols = slice(p * LANES, (p + 1) * LANES), slice(j * SB_TILE, (j + 1) * SB_TILE)
            qt_ref[0, p, j] = qt[rows, cols]
            vt_ref[0, p, j] = vt[rows, cols]
    k_ref[0] = _dot(h, win_ref[:, sbw:2 * sbw]).astype(BF16)

    ugc = _dot(h, win_ref[:, o_conv:o_ga])
    cu = ugc[:, 2 * cw:3 * cw] * ugc[:, 0:cw]
    gate_b = ugc[:, cw:2 * cw]
    cu_buf[halo:halo + tm, :] = cu
    conv = cw_ref[2:3, :] * cu
    for i in range(CONV_K - 1):
        shift = CONV_K - 1 - i
        conv = conv + cw_ref[i:i + 1, :] * cu_buf[halo - shift:halo - shift + tm, :]
    cu_buf[0:halo, :] = cu[tm - halo:tm, :]
    yb = (gate_b * conv).astype(BF16)
    branch_b = _dot(yb, wbb_ref[...])
    mb_ref[0] = (jax.nn.sigmoid(_dot(h, win_ref[:, o_gb:o_gb + d])) * branch_b).astype(BF16)
    sa_ref[0] = jax.nn.sigmoid(_dot(h, win_ref[:, o_ga:o_gb])).astype(BF16)


def _mix_in(x, g, win, cw, wbb):
    b, s, d = x.shape
    tm = TOKEN_TILE
    sbw = SB_HEADS * SB_HEAD_DIM
    cwid = wbb.shape[0]
    assert win.shape[1] == 3 * sbw + 3 * cwid + 2 * d
    pairs = sbw // LANES
    tiled = jax.ShapeDtypeStruct((b, pairs, s // SB_TILE, LANES, SB_TILE), BF16)
    out_shape = (
        tiled,
        jax.ShapeDtypeStruct((b, s, sbw), BF16),
        tiled,
        jax.ShapeDtypeStruct((b, s, d), BF16),
        jax.ShapeDtypeStruct((b, s, d), BF16),
    )
    row = lambda bi, i: (bi, i, 0)
    tiled_spec = pl.BlockSpec((1, pairs, tm // SB_TILE, LANES, SB_TILE),
                              lambda bi, i: (bi, 0, i, 0, 0))
    return pl.pallas_call(
        _mix_in_kernel,
        grid=(b, s // tm),
        in_specs=[
            pl.BlockSpec((1, tm, d), row),
            _const_spec(g.shape), _const_spec(win.shape), _const_spec(cw.shape),
            _const_spec(wbb.shape),
        ],
        out_specs=(
            tiled_spec,
            pl.BlockSpec((1, tm, sbw), row),
            tiled_spec,
            pl.BlockSpec((1, tm, d), row),
            pl.BlockSpec((1, tm, d), row),
        ),
        out_shape=out_shape,
        scratch_shapes=[pltpu.VMEM((tm + SUBLANES, cwid), F32)],
        compiler_params=pltpu.CompilerParams(
            dimension_semantics=("arbitrary", "arbitrary"),
            vmem_limit_bytes=48 * MIB),
        name="mix_in",
    )(x, g, win, cw, wbb)


SB_STAGES = 5
SB_SLOTS = 4


def _sb_attn_kernel(qt_ref, k_ref, vt_ref, tri_ref, o_ref,
                    qm_ref, c_ref, acc_ref, z_buf, p_buf, cum_buf, w_buf):
    nq, tq = qt_ref.shape[2], qt_ref.shape[4]
    tk = SB_TILE
    hd = SB_HEAD_DIM
    dummy = nq

    def prep(i, carry):
        qt = qt_ref[0, 0, i]
        zero = jnp.zeros((hd, tq), BF16)
        qm_ref[0, i, 0:hd, :] = qt[0:hd]
        qm_ref[0, i, hd:2 * hd, :] = zero
        qm_ref[1, i, 0:hd, :] = zero
        qm_ref[1, i, hd:2 * hd, :] = qt[hd:2 * hd]
        return carry

    lax.fori_loop(0, nq, prep, 0)
    qm_ref[:, dummy] = jnp.zeros((2, 2 * hd, tq), BF16)
    c_ref[...] = jnp.zeros(c_ref.shape, F32)
    acc_ref[...] = jnp.zeros(acc_ref.shape, F32)
    @pl.when(jnp.logical_and(pl.program_id(0) == 0, pl.program_id(1) == 0))
    def _():
        z_buf[...] = jnp.zeros(z_buf.shape, F32)
        p_buf[...] = jnp.zeros(p_buf.shape, BF16)
        cum_buf[...] = jnp.zeros(cum_buf.shape, F32)
        w_buf[...] = jnp.zeros(w_buf.shape, BF16)

    key = lax.broadcasted_iota(jnp.int32, (tk, tq), 0)
    qry = lax.broadcasted_iota(jnp.int32, (tk, tq), 1)
    causal = key < qry

    def stage_ma(i, j, h, slot):
        kt = k_ref[0, pl.ds(pl.multiple_of(j * tk, tk), tk), :]
        z_buf[slot] = _dot(kt, qm_ref[h, i])

    def stage_ea(slot, masked):
        z = z_buf[slot]
        if masked:
            z = jnp.where(causal, z, MASKED)
            z_buf[slot] = z
        p = jnp.maximum(z, 0.0) + jnp.log(1.0 + jnp.exp2(-jnp.abs(z))) * LOG2E
        p_buf[slot] = p.astype(BF16)

    def stage_mb(slot):
        cum_buf[slot] = _dot(tri_ref[...], p_buf[slot])

    def stage_eb(i, h, slot):
        cum = cum_buf[slot]
        c = c_ref[h, i, 0:1, :]
        w_buf[slot] = jnp.exp2(z_buf[slot] - cum + c).astype(BF16)
        c_ref[h, i] = jnp.broadcast_to(c - cum[0:1, :], (SUBLANES, tq))

    def stage_mc(i, j, h, slot):
        vt = vt_ref[0, 0, j, h * hd:(h + 1) * hd, :]
        acc_ref[h, i] += _dot(vt, w_buf[slot])

    def run_distance(d, masked):
        n_items = 2 * (nq - d)

        def item(m):
            valid = jnp.logical_and(m >= 0, m < n_items)
            t = m // 2
            return jnp.where(valid, d + t, dummy), jnp.where(valid, t, 0)

        def body(u, carry):
            for r in range(SB_SLOTS):
                s = SB_SLOTS * u + r
                i4, j4 = item(s - 4)
                stage_mc(i4, j4, (r - 4) % 2, (r - 4) % SB_SLOTS)
                i3, _ = item(s - 3)
                stage_eb(i3, (r - 3) % 2, (r - 3) % SB_SLOTS)
                stage_mb((r - 2) % SB_SLOTS)
                stage_ea((r - 1) % SB_SLOTS, masked)
                i0, j0 = item(s)
                stage_ma(i0, j0, r % 2, r)
            return carry

        steps = n_items + SB_STAGES - 1
        lax.fori_loop(0, (steps + SB_SLOTS - 1) // SB_SLOTS, body, 0)

    def alive_after(d):
        cc = jnp.maximum(c_ref[0, 0:nq], c_ref[1, 0:nq])
        blk = lax.broadcasted_iota(jnp.int32, cc.shape, 0)
        top = jnp.max(jnp.where(blk > d, cc, MASKED), axis=0)
        return (jnp.max(top) > DEAD_LOG2).astype(jnp.int32)

    run_distance(0, True)

    def more(state):
        d, alive = state
        return jnp.logical_and(d < nq, alive > 0)

    def step(state):
        d, _ = state
        run_distance(d, False)
        return d + 1, alive_after(d)

    lax.while_loop(more, step, (jnp.int32(1), alive_after(0)))

    def emit(i, carry):
        both = jnp.concatenate([acc_ref[0, i], acc_ref[1, i]], axis=0)
        o_ref[0, pl.ds(pl.multiple_of(i * tq, tq), tq), :] = both.T.astype(BF16)
        return carry

    lax.fori_loop(0, nq, emit, 0, unroll=4)


def _sb_attn(qt, k, vt, tri):
    b, pairs, nq, _, tq = qt.shape
    s, sbw = k.shape[1], k.shape[2]
    tk = SB_TILE
    return pl.pallas_call(
        _sb_attn_kernel,
        grid=(b, pairs),
        in_specs=[
            pl.BlockSpec((1, 1, nq, LANES, tq), lambda bi, p: (bi, p, 0, 0, 0)),
            pl.BlockSpec((1, s, LANES), lambda bi, p: (bi, 0, p)),
            pl.BlockSpec((1, 1, s // tk, LANES, tk), lambda bi, p: (bi, p, 0, 0, 0)),
            _const_spec(tri.shape),
        ],
        out_specs=pl.BlockSpec((1, s, LANES), lambda bi, p: (bi, 0, p)),
        out_shape=jax.ShapeDtypeStruct((b, s, sbw), BF16),
        scratch_shapes=[
            pltpu.VMEM((2, nq + 1, LANES, tq), BF16),
            pltpu.VMEM((2, nq + 1, SUBLANES, tq), F32),
            pltpu.VMEM((2, nq + 1, SB_HEAD_DIM, tq), F32),
            pltpu.VMEM((SB_SLOTS, tk, tq), F32),
            pltpu.VMEM((SB_SLOTS, tk, tq), BF16),
            pltpu.VMEM((SB_SLOTS, tk, tq), F32),
            pltpu.VMEM((SB_SLOTS, tk, tq), BF16),
        ],
        compiler_params=pltpu.CompilerParams(
            dimension_semantics=("arbitrary", "arbitrary"),
            vmem_limit_bytes=32 * MIB),
        name="sb_attn",
    )(qt, k, vt, tri)


def _mem_kv_kernel(mem_ref, g_ref, wkv_ref, kt_ref, v_ref):
    d = mem_ref.shape[2]
    hm = _rms(mem_ref[0], g_ref[...]).astype(BF16)
    kt = _dot_tn(wkv_ref[:, 0:d], hm).astype(BF16)
    dh = kt_ref.shape[2]
    for hd in range(kt_ref.shape[1]):
        kt_ref[0, hd] = kt[hd * dh:(hd + 1) * dh, :]
    v_ref[0] = _dot(hm, wkv_ref[:, d:2 * d]).astype(BF16)


def _mem_kv(mem, g, wkv):
    b, m, d = mem.shape
    dh = d // MEM_HEADS
    return pl.pallas_call(
        _mem_kv_kernel,
        grid=(b,),
        in_specs=[pl.BlockSpec((1, m, d), lambda bi: (bi, 0, 0)),
                  _const_spec(g.shape), _const_spec(wkv.shape)],
        out_specs=(pl.BlockSpec((1, MEM_HEADS, dh, m), lambda bi: (bi, 0, 0, 0)),
                   pl.BlockSpec((1, m, d), lambda bi: (bi, 0, 0))),
        out_shape=(jax.ShapeDtypeStruct((b, MEM_HEADS, dh, m), BF16),
                   jax.ShapeDtypeStruct((b, m, d), BF16)),
        compiler_params=pltpu.CompilerParams(
            dimension_semantics=("parallel",), vmem_limit_bytes=32 * MIB),
        name="mem_kv",
    )(mem, g, wkv)


def _mix_out_kernel(x_ref, oa_ref, sa_ref, mb_ref, wba_ref, wmo_ref, gq_ref, wmq_ref,
                    kmt_ref, vm_ref, wmemo_ref, o_ref):
    branch_a = _dot(oa_ref[0], wba_ref[...])
    merged = (sa_ref[0].astype(F32) * branch_a + mb_ref[0].astype(F32)).astype(BF16)
    x1 = x_ref[0] + _dot(merged, wmo_ref[...])

    dh = kmt_ref.shape[2]
    hq = _rms(x1, gq_ref[...]).astype(BF16)
    qm = (_dot(hq, wmq_ref[...]) * (dh ** -0.5)).astype(BF16)
    heads = []
    for hd in range(MEM_HEADS):
        sc = _dot(qm[:, hd * dh:(hd + 1) * dh], kmt_ref[0, hd])
        e = jnp.exp(sc - jnp.max(sc, axis=-1, keepdims=True))
        denom = jnp.sum(e, axis=-1, keepdims=True)
        o = _dot(e.astype(BF16), vm_ref[0, :, hd * dh:(hd + 1) * dh]) / denom
        heads.append(o.astype(BF16))
    o_ref[0] = x1 + _dot(jnp.concatenate(heads, axis=1), wmemo_ref[...])


def _mix_out(x, oa, sa, mb, wba, wmo, gq, wmq, kmt, vm, wmemo):
    b, s, d = x.shape
    tm = TOKEN_TILE
    row = lambda bi, i: (bi, i, 0)
    return pl.pallas_call(
        _mix_out_kernel,
        grid=(b, s // tm),
        in_specs=[
            pl.BlockSpec((1, tm, d), row),
            pl.BlockSpec((1, tm, oa.shape[2]), row),
            pl.BlockSpec((1, tm, d), row),
            pl.BlockSpec((1, tm, d), row),
            _const_spec(wba.shape), _const_spec(wmo.shape), _const_spec(gq.shape),
            _const_spec(wmq.shape),
            pl.BlockSpec((1,) + kmt.shape[1:], lambda bi, i: (bi, 0, 0, 0)),
            pl.BlockSpec((1,) + vm.shape[1:], lambda bi, i: (bi, 0, 0)),
            _const_spec(wmemo.shape),
        ],
        out_specs=pl.BlockSpec((1, tm, d), row),
        out_shape=jax.ShapeDtypeStruct((b, s, d), F32),
        compiler_params=pltpu.CompilerParams(
            dimension_semantics=("parallel", "parallel"), vmem_limit_bytes=48 * MIB),
        name="mix_out",
    )(x, oa, sa, mb, wba, wmo, gq, wmq, kmt, vm, wmemo)


def _ffn_kernel(x_ref, gf_ref, win_ref, wo_ref, gfin_ref, o_ref, *, chunks, final_norm):
    x = x_ref[0]
    h = _rms(x, gf_ref[...]).astype(BF16)
    hidden = wo_ref.shape[0]
    tiles = pl.cdiv(hidden, MXU_DIM)
    bounds = [min(hidden, (c * tiles // chunks) * MXU_DIM) for c in range(chunks + 1)]
    y = x
    for c in range(chunks):
        lo, hi = bounds[c], bounds[c + 1]
        gate = _dot(h, win_ref[:, lo:hi])
        up = _dot(h, win_ref[:, hidden + lo:hidden + hi])
        act = (gate * jax.nn.sigmoid(gate) * up).astype(BF16)
        y = y + _dot(act, wo_ref[lo:hi, :])
    o_ref[0] = _rms(y, gfin_ref[...]) if final_norm else y


def _ffn(x, gf, win, wo, gfin, final_norm):
    b, s, d = x.shape
    tm = TOKEN_TILE
    row = lambda bi, i: (bi, i, 0)
    chunks = 2
    assert win.shape[1] == 2 * wo.shape[0] and wo.shape[0] % LANES == 0
    return pl.pallas_call(
        functools.partial(_ffn_kernel, chunks=chunks, final_norm=final_norm),
        grid=(b, s // tm),
        in_specs=[pl.BlockSpec((1, tm, d), row), _const_spec(gf.shape), _const_spec(win.shape),
                  _const_spec(wo.shape), _const_spec(gfin.shape)],
        out_specs=pl.BlockSpec((1, tm, d), row),
        out_shape=jax.ShapeDtypeStruct((b, s, d), F32),
        compiler_params=pltpu.CompilerParams(
            dimension_semantics=("parallel", "parallel"), vmem_limit_bytes=56 * MIB),
        name="ffn",
    )(x, gf, win, wo, gfin)


def kernel(x, mem, norm_mix, w_in, conv_w, w_branch_a, w_branch_b, w_mix_out, norm_mem_q,
           norm_mem_kv, w_mem_q, w_mem_kv, w_mem_o, norm_ffn, w_ffn_in, w_ffn_out, norm_final):
    depth = w_in.shape[0]
    assert x.shape[1] % TOKEN_TILE == 0 and TOKEN_TILE % SB_TILE == 0
    assert LANES == 2 * SB_HEAD_DIM

    idx = jnp.arange(SB_TILE)
    tri = (idx[None, :] >= idx[:, None]).astype(BF16)

    for l in range(depth):
        qt, k, vt, sa, mb = _mix_in(x, norm_mix[l][None, :], w_in[l].astype(BF16), conv_w[l],
                                    w_branch_b[l].astype(BF16))
        oa = _sb_attn(qt, k, vt, tri)
        kmt, vm = _mem_kv(mem, norm_mem_kv[l][None, :], w_mem_kv[l].astype(BF16))
        x = _mix_out(x, oa, sa, mb, w_branch_a[l].astype(BF16), w_mix_out[l].astype(BF16),
                     norm_mem_q[l][None, :], w_mem_q[l].astype(BF16), kmt, vm,
                     w_mem_o[l].astype(BF16))
        x = _ffn(x, norm_ffn[l][None, :], w_ffn_in[l].astype(BF16), w_ffn_out[l].astype(BF16),
                 norm_final[None, :], final_norm=(l == depth - 1))
    return x
```

```python
import functools
import math

import jax
import jax.numpy as jnp
from jax import lax
from jax.experimental import pallas as pl
from jax.experimental.pallas import tpu as pltpu

EPS = 1e-6
SB_HEADS = 8
SB_HEAD_DIM = 64
CONV_K = 3
MEM_HEADS = 4

LANES = 128
SUBLANES = 8
MXU_DIM = 256
TOKEN_TILE = 512
SB_TILE = 256
MIB = 1024 * 1024

LOG2E = math.log2(math.e)
MASKED = -1e30
DEAD_LOG2 = -160.0

BF16 = jnp.bfloat16
F32 = jnp.float32


def _dot(a, b):
    return jnp.dot(a, b, preferred_element_type=F32)


def _dot_tn(a, b):
    return lax.dot_general(a, b, (((0,), (1,)), ((), ())), preferred_element_type=F32)


def _bf16(w):
    return w.astype(BF16)


def _rms(x, g):
    var = jnp.mean(x * x, axis=-1, keepdims=True)
    return x * lax.rsqrt(var + EPS) * g


def _const_spec(shape):
    zeros = (0,) * len(shape)
    return pl.BlockSpec(shape, lambda *_: zeros, pipeline_mode=pl.Buffered(1))


def _mix_in_kernel(x_ref, g_ref, win_ref, cw_ref, wbb_ref,
                   qt_ref, k_ref, vt_ref, sa_ref, mb_ref, cu_buf):
    tm = x_ref.shape[1]
    d = x_ref.shape[2]
    cw = wbb_ref.shape[0]
    sbw = k_ref.shape[2]
    halo = SUBLANES
    o_conv = 3 * sbw
    o_ga = o_conv + 3 * cw
    o_gb = o_ga + d

    @pl.when(pl.program_id(1) == 0)
    def _():
        cu_buf[0:halo, :] = jnp.zeros((halo, cw), F32)

    h = _rms(x_ref[0], g_ref[...]).astype(BF16)

    qt = _dot_tn(_bf16(win_ref[:, 0:sbw]), h) * (LOG2E * SB_HEAD_DIM ** -0.5)
    qt = qt.astype(BF16)
    vt = _dot_tn(_bf16(win_ref[:, 2 * sbw:3 * sbw]), h).astype(BF16)
    for p in range(qt_ref.shape[1]):
        for j in range(qt_ref.shape[2]):
            rows, cols = slice(p * LANES, (p + 1) * LANES), slice(j * SB_TILE, (j + 1) * SB_TILE)
            qt_ref[0, p, j] = qt[rows, cols]
            vt_ref[0, p, j] = vt[rows, cols]
    k_ref[0] = _dot(h, _bf16(win_ref[:, sbw:2 * sbw])).astype(BF16)

    ugc = _dot(h, _bf16(win_ref[:, o_conv:o_ga]))
    cu = ugc[:, 2 * cw:3 * cw] * ugc[:, 0:cw]
    gate_b = ugc[:, cw:2 * cw]
    cu_buf[halo:halo + tm, :] = cu
    conv = cw_ref[2:3, :] * cu
    for i in range(CONV_K - 1):
        shift = CONV_K - 1 - i
        conv = conv + cw_ref[i:i + 1, :] * cu_buf[halo - shift:halo - shift + tm, :]
    cu_buf[0:halo, :] = cu[tm - halo:tm, :]
    yb = (gate_b * conv).astype(BF16)
    branch_b = _dot(yb, _bf16(wbb_ref[...]))
    mb_ref[0] = (jax.nn.sigmoid(_dot(h, _bf16(win_ref[:, o_gb:o_gb + d]))) * branch_b).astype(BF16)
    sa_ref[0] = jax.nn.sigmoid(_dot(h, _bf16(win_ref[:, o_ga:o_gb]))).astype(BF16)


def _mix_in(x, g, win, cw, wbb):
    b, s, d = x.shape
    tm = TOKEN_TILE
    sbw = SB_HEADS * SB_HEAD_DIM
    cwid = wbb.shape[0]
    assert win.shape[1] == 3 * sbw + 3 * cwid + 2 * d
    pairs = sbw // LANES
    tiled = jax.ShapeDtypeStruct((b, pairs, s // SB_TILE, LANES, SB_TILE), BF16)
    out_shape = (
        tiled,
        jax.ShapeDtypeStruct((b, s, sbw), BF16),
        tiled,
        jax.ShapeDtypeStruct((b, s, d), BF16),
        jax.ShapeDtypeStruct((b, s, d), BF16),
    )
    row = lambda bi, i: (bi, i, 0)
    tiled_spec = pl.BlockSpec((1, pairs, tm // SB_TILE, LANES, SB_TILE),
                              lambda bi, i: (bi, 0, i, 0, 0))
    return pl.pallas_call(
        _mix_in_kernel,
        grid=(b, s // tm),
        in_specs=[
            pl.BlockSpec((1, tm, d), row),
            _const_spec(g.shape), _const_spec(win.shape), _const_spec(cw.shape),
            _const_spec(wbb.shape),
        ],
        out_specs=(
            tiled_spec,
            pl.BlockSpec((1, tm, sbw), row),
            tiled_spec,
            pl.BlockSpec((1, tm, d), row),
            pl.BlockSpec((1, tm, d), row),
        ),
        out_shape=out_shape,
        scratch_shapes=[pltpu.VMEM((tm + SUBLANES, cwid), F32)],
        compiler_params=pltpu.CompilerParams(
            dimension_semantics=("arbitrary", "arbitrary"),
            vmem_limit_bytes=48 * MIB),
        name="mix_in",
    )(x, g, win, cw, wbb)


SB_STAGES = 5
SB_SLOTS = 4


def _sb_attn_kernel(qt_ref, k_ref, vt_ref, tri_ref, o_ref,
                    qm_ref, c_ref, acc_ref, z_buf, p_buf, cum_buf, w_buf):
    nq, tq = qt_ref.shape[2], qt_ref.shape[4]
    tk = SB_TILE
    hd = SB_HEAD_DIM
    dummy = nq

    def prep(i, carry):
        qt = qt_ref[0, 0, i]
        zero = jnp.zeros((hd, tq), BF16)
        qm_ref[0, i, 0:hd, :] = qt[0:hd]
        qm_ref[0, i, hd:2 * hd, :] = zero
        qm_ref[1, i, 0:hd, :] = zero
        qm_ref[1, i, hd:2 * hd, :] = qt[hd:2 * hd]
        return carry

    lax.fori_loop(0, nq, prep, 0)
    qm_ref[:, dummy] = jnp.zeros((2, 2 * hd, tq), BF16)
    c_ref[...] = jnp.zeros(c_ref.shape, F32)
    acc_ref[...] = jnp.zeros(acc_ref.shape, F32)
    @pl.when(jnp.logical_and(pl.program_id(0) == 0, pl.program_id(1) == 0))
    def _():
        z_buf[...] = jnp.zeros(z_buf.shape, F32)
        p_buf[...] = jnp.zeros(p_buf.shape, BF16)
        cum_buf[...] = jnp.zeros(cum_buf.shape, F32)
        w_buf[...] = jnp.zeros(w_buf.shape, BF16)

    key = lax.broadcasted_iota(jnp.int32, (tk, tq), 0)
    qry = lax.broadcasted_iota(jnp.int32, (tk, tq), 1)
    causal = key < qry

    def stage_ma(i, j, h, slot):
        kt = k_ref[0, pl.ds(pl.multiple_of(j * tk, tk), tk), :]
        z_buf[slot] = _dot(kt, qm_ref[h, i])

    def softplus2(z):
        return jnp.maximum(z, 0.0) + jnp.log(1.0 + jnp.exp2(-jnp.abs(z))) * LOG2E

    lo, hi = slice(0, tk // 2), slice(tk // 2, tk)
    live_quadrants = ((lo, lo, True), (lo, hi, False), (hi, hi, True))
    causal_q = causal[lo, lo]

    def stage_ea(slot, masked):
        if not masked:
            p_buf[slot] = softplus2(z_buf[slot]).astype(BF16)
            return
        for ks, qs, triangle in live_quadrants:
            z = z_buf[slot, ks, qs]
            if triangle:
                z = jnp.where(causal_q, z, MASKED)
                z_buf[slot, ks, qs] = z
            p_buf[slot, ks, qs] = softplus2(z).astype(BF16)
        p_buf[slot, hi, lo] = jnp.zeros((tk // 2, tq // 2), BF16)

    def stage_mb(slot):
        cum_buf[slot] = _dot(tri_ref[...], p_buf[slot])

    def stage_eb(i, h, slot, masked):
        if not masked:
            cum = cum_buf[slot]
            c = c_ref[h, i, 0:1, :]
            w_buf[slot] = jnp.exp2(z_buf[slot] - cum + c).astype(BF16)
            c_ref[h, i] = jnp.broadcast_to(c - cum[0:1, :], (SUBLANES, tq))
            return
        for ks, qs, _ in live_quadrants:
            w_buf[slot, ks, qs] = jnp.exp2(z_buf[slot, ks, qs] - cum_buf[slot, ks, qs]).astype(BF16)
        w_buf[slot, hi, lo] = jnp.zeros((tk // 2, tq // 2), BF16)
        c_ref[h, i] = jnp.broadcast_to(-cum_buf[slot, 0:1, :], (SUBLANES, tq))

    def stage_mc(i, j, h, slot):
        vt = vt_ref[0, 0, j, h * hd:(h + 1) * hd, :]
        acc_ref[h, i] += _dot(vt, w_buf[slot])

    def run_distance(d, masked):
        n_items = 2 * (nq - d)

        def item(m):
            valid = jnp.logical_and(m >= 0, m < n_items)
            t = m // 2
            return jnp.where(valid, d + t, dummy), jnp.where(valid, t, 0)

        def body(u, carry):
            for r in range(SB_SLOTS):
                s = SB_SLOTS * u + r
                i4, j4 = item(s - 4)
                stage_mc(i4, j4, (r - 4) % 2, (r - 4) % SB_SLOTS)
                i3, _ = item(s - 3)
                stage_eb(i3, (r - 3) % 2, (r - 3) % SB_SLOTS, masked)
                stage_mb((r - 2) % SB_SLOTS)
                stage_ea((r - 1) % SB_SLOTS, masked)
                i0, j0 = item(s)
                stage_ma(i0, j0, r % 2, r)
            return carry

        steps = n_items + SB_STAGES - 1
        lax.fori_loop(0, (steps + SB_SLOTS - 1) // SB_SLOTS, body, 0)

    def alive_after(d):
        cc = jnp.maximum(c_ref[0, 0:nq], c_ref[1, 0:nq])
        blk = lax.broadcasted_iota(jnp.int32, cc.shape, 0)
        top = jnp.max(jnp.where(blk > d, cc, MASKED), axis=0)
        return (jnp.max(top) > DEAD_LOG2).astype(jnp.int32)

    run_distance(0, True)

    def more(state):
        d, alive = state
        return jnp.logical_and(d < nq, alive > 0)

    def step(state):
        d, _ = state
        run_distance(d, False)
        return d + 1, alive_after(d)

    lax.while_loop(more, step, (jnp.int32(1), alive_after(0)))

    def emit(i, carry):
        both = jnp.concatenate([acc_ref[0, i], acc_ref[1, i]], axis=0)
        o_ref[0, pl.ds(pl.multiple_of(i * tq, tq), tq), :] = both.T.astype(BF16)
        return carry

    lax.fori_loop(0, nq, emit, 0, unroll=4)


def _sb_attn(qt, k, vt, tri):
    b, pairs, nq, _, tq = qt.shape
    s, sbw = k.shape[1], k.shape[2]
    tk = SB_TILE
    return pl.pallas_call(
        _sb_attn_kernel,
        grid=(b, pairs),
        in_specs=[
            pl.BlockSpec((1, 1, nq, LANES, tq), lambda bi, p: (bi, p, 0, 0, 0)),
            pl.BlockSpec((1, s, LANES), lambda bi, p: (bi, 0, p)),
            pl.BlockSpec((1, 1, s // tk, LANES, tk), lambda bi, p: (bi, p, 0, 0, 0)),
            _const_spec(tri.shape),
        ],
        out_specs=pl.BlockSpec((1, s, LANES), lambda bi, p: (bi, 0, p)),
        out_shape=jax.ShapeDtypeStruct((b, s, sbw), BF16),
        scratch_shapes=[
            pltpu.VMEM((2, nq + 1, LANES, tq), BF16),
            pltpu.VMEM((2, nq + 1, SUBLANES, tq), F32),
            pltpu.VMEM((2, nq + 1, SB_HEAD_DIM, tq), F32),
            pltpu.VMEM((SB_SLOTS, tk, tq), F32),
            pltpu.VMEM((SB_SLOTS, tk, tq), BF16),
            pltpu.VMEM((SB_SLOTS, tk, tq), F32),
            pltpu.VMEM((SB_SLOTS, tk, tq), BF16),
        ],
        compiler_params=pltpu.CompilerParams(
            dimension_semantics=("arbitrary", "arbitrary"),
            vmem_limit_bytes=32 * MIB),
        name="sb_attn",
    )(qt, k, vt, tri)


def _mem_kv_kernel(mem_ref, g_ref, wkv_ref, kt_ref, v_ref):
    d = mem_ref.shape[2]
    hm = _rms(mem_ref[0], g_ref[...]).astype(BF16)
    kt = _dot_tn(_bf16(wkv_ref[:, 0:d]), hm).astype(BF16)
    dh = kt_ref.shape[2]
    for hd in range(kt_ref.shape[1]):
        kt_ref[0, hd] = kt[hd * dh:(hd + 1) * dh, :]
    v_ref[0] = _dot(hm, _bf16(wkv_ref[:, d:2 * d])).astype(BF16)


def _mem_kv(mem, g, wkv):
    b, m, d = mem.shape
    dh = d // MEM_HEADS
    return pl.pallas_call(
        _mem_kv_kernel,
        grid=(b,),
        in_specs=[pl.BlockSpec((1, m, d), lambda bi: (bi, 0, 0)),
                  _const_spec(g.shape), _const_spec(wkv.shape)],
        out_specs=(pl.BlockSpec((1, MEM_HEADS, dh, m), lambda bi: (bi, 0, 0, 0)),
                   pl.BlockSpec((1, m, d), lambda bi: (bi, 0, 0))),
        out_shape=(jax.ShapeDtypeStruct((b, MEM_HEADS, dh, m), BF16),
                   jax.ShapeDtypeStruct((b, m, d), BF16)),
        compiler_params=pltpu.CompilerParams(
            dimension_semantics=("parallel",), vmem_limit_bytes=32 * MIB),
        name="mem_kv",
    )(mem, g, wkv)


def _mix_out_kernel(x_ref, oa_ref, sa_ref, mb_ref, wba_ref, wmo_ref, gq_ref, wmq_ref,
                    kmt_ref, vm_ref, wmemo_ref, o_ref):
    branch_a = _dot(oa_ref[0], _bf16(wba_ref[...]))
    merged = (sa_ref[0].astype(F32) * branch_a + mb_ref[0].astype(F32)).astype(BF16)
    x1 = x_ref[0] + _dot(merged, _bf16(wmo_ref[...]))

    dh = kmt_ref.shape[2]
    hq = _rms(x1, gq_ref[...]).astype(BF16)
    qm = (_dot(hq, _bf16(wmq_ref[...])) * (dh ** -0.5)).astype(BF16)
    heads = []
    for hd in range(MEM_HEADS):
        sc = _dot(qm[:, hd * dh:(hd + 1) * dh], kmt_ref[0, hd])
        e = jnp.exp(sc - jnp.max(sc, axis=-1, keepdims=True))
        denom = jnp.sum(e, axis=-1, keepdims=True)
        o = _dot(e.astype(BF16), vm_ref[0, :, hd * dh:(hd + 1) * dh]) / denom
        heads.append(o.astype(BF16))
    o_ref[0] = x1 + _dot(jnp.concatenate(heads, axis=1), _bf16(wmemo_ref[...]))


def _mix_out(x, oa, sa, mb, wba, wmo, gq, wmq, kmt, vm, wmemo):
    b, s, d = x.shape
    tm = TOKEN_TILE
    row = lambda bi, i: (bi, i, 0)
    return pl.pallas_call(
        _mix_out_kernel,
        grid=(b, s // tm),
        in_specs=[
            pl.BlockSpec((1, tm, d), row),
            pl.BlockSpec((1, tm, oa.shape[2]), row),
            pl.BlockSpec((1, tm, d), row),
            pl.BlockSpec((1, tm, d), row),
            _const_spec(wba.shape), _const_spec(wmo.shape), _const_spec(gq.shape),
            _const_spec(wmq.shape),
            pl.BlockSpec((1,) + kmt.shape[1:], lambda bi, i: (bi, 0, 0, 0)),
            pl.BlockSpec((1,) + vm.shape[1:], lambda bi, i: (bi, 0, 0)),
            _const_spec(wmemo.shape),
        ],
        out_specs=pl.BlockSpec((1, tm, d), row),
        out_shape=jax.ShapeDtypeStruct((b, s, d), F32),
        compiler_params=pltpu.CompilerParams(
            dimension_semantics=("parallel", "parallel"), vmem_limit_bytes=48 * MIB),
        name="mix_out",
    )(x, oa, sa, mb, wba, wmo, gq, wmq, kmt, vm, wmemo)


def _ffn_kernel(x_ref, gf_ref, win_ref, wo_ref, gfin_ref, o_ref, *, chunks, final_norm):
    hidden = wo_ref.shape[0]
    tiles = pl.cdiv(hidden, MXU_DIM)
    bounds = [min(hidden, (c * tiles // chunks) * MXU_DIM) for c in range(chunks + 1)]
    x = x_ref[0]
    h = _rms(x, gf_ref[...]).astype(BF16)
    y = x
    for c in range(chunks):
        lo, hi = bounds[c], bounds[c + 1]
        gate = _dot(h, _bf16(win_ref[:, lo:hi]))
        up = _dot(h, _bf16(win_ref[:, hidden + lo:hidden + hi]))
        act = (gate * jax.nn.sigmoid(gate) * up).astype(BF16)
        y = y + _dot(act, _bf16(wo_ref[lo:hi, :]))
    o_ref[0] = _rms(y, gfin_ref[...]) if final_norm else y


def _ffn(x, gf, win, wo, gfin, final_norm):
    b, s, d = x.shape
    tm = TOKEN_TILE
    row = lambda bi, i: (bi, i, 0)
    chunks = 2
    assert win.shape[1] == 2 * wo.shape[0] and wo.shape[0] % LANES == 0
    return pl.pallas_call(
        functools.partial(_ffn_kernel, chunks=chunks, final_norm=final_norm),
        grid=(b, s // tm),
        in_specs=[pl.BlockSpec((1, tm, d), row), _const_spec(gf.shape), _const_spec(win.shape),
                  _const_spec(wo.shape), _const_spec(gfin.shape)],
        out_specs=pl.BlockSpec((1, tm, d), row),
        out_shape=jax.ShapeDtypeStruct((b, s, d), F32),
        compiler_params=pltpu.CompilerParams(
            dimension_semantics=("parallel", "parallel"), vmem_limit_bytes=56 * MIB),
        name="ffn",
    )(x, gf, win, wo, gfin)


def kernel(x, mem, norm_mix, w_in, conv_w, w_branch_a, w_branch_b, w_mix_out, norm_mem_q,
           norm_mem_kv, w_mem_q, w_mem_kv, w_mem_o, norm_ffn, w_ffn_in, w_ffn_out, norm_final):
    depth = w_in.shape[0]
    assert x.shape[1] % TOKEN_TILE == 0 and TOKEN_TILE % SB_TILE == 0
    assert LANES == 2 * SB_HEAD_DIM

    idx = jnp.arange(SB_TILE)
    tri = (idx[None, :] >= idx[:, None]).astype(BF16)

    for l in range(depth):
        qt, k, vt, sa, mb = _mix_in(x, norm_mix[l][None, :], w_in[l], conv_w[l], w_branch_b[l])
        oa = _sb_attn(qt, k, vt, tri)
        kmt, vm = _mem_kv(mem, norm_mem_kv[l][None, :], w_mem_kv[l])
        x = _mix_out(x, oa, sa, mb, w_branch_a[l], w_mix_out[l], norm_mem_q[l][None, :],
                     w_mem_q[l], kmt, vm, w_mem_o[l])
        x = _ffn(x, norm_ffn[l][None, :], w_ffn_in[l], w_ffn_out[l],
                 norm_final[None, :], final_norm=(l == depth - 1))
    return x
```

```python
import functools
import math

import jax
import jax.numpy as jnp
from jax import lax
from jax.experimental import pallas as pl
from jax.experimental.pallas import tpu as pltpu

EPS = 1e-6
SB_HEADS = 8
SB_HEAD_DIM = 64
CONV_K = 3
MEM_HEADS = 4

LANES = 128
SUBLANES = 8
MXU_DIM = 256
TOKEN_TILE = 512
SB_TILE = 256
MIB = 1024 * 1024

LOG2E = math.log2(math.e)
MASKED = -1e30
DEAD_LOG2 = -160.0

BF16 = jnp.bfloat16
F32 = jnp.float32


def _dot(a, b):
    return jnp.dot(a, b, preferred_element_type=F32)


def _dot_tn(a, b):
    return lax.dot_general(a, b, (((0,), (1,)), ((), ())), preferred_element_type=F32)


def _bf16(w):
    return w.astype(BF16)


def _rms(x, g):
    var = jnp.mean(x * x, axis=-1, keepdims=True)
    return x * lax.rsqrt(var + EPS) * g


def _const_spec(shape):
    zeros = (0,) * len(shape)
    return pl.BlockSpec(shape, lambda *_: zeros, pipeline_mode=pl.Buffered(1))


def _mix_in_kernel(x_ref, g_ref, win_ref, cw_ref, wbb_ref,
                   qt_ref, k_ref, vt_ref, sa_ref, mb_ref, cu_buf):
    tm = x_ref.shape[1]
    d = x_ref.shape[2]
    cw = wbb_ref.shape[0]
    sbw = k_ref.shape[2]
    halo = SUBLANES
    o_conv = 3 * sbw
    o_ga = o_conv + 3 * cw
    o_gb = o_ga + d

    @pl.when(pl.program_id(1) == 0)
    def _():
        cu_buf[0:halo, :] = jnp.zeros((halo, cw), F32)

    h = _rms(x_ref[0], g_ref[...]).astype(BF16)

    ugc = _dot(h, _bf16(win_ref[:, o_conv:o_ga]))
    cu = ugc[:, 2 * cw:3 * cw] * ugc[:, 0:cw]
    gate_b = ugc[:, cw:2 * cw]
    cu_buf[halo:halo + tm, :] = cu
    conv = cw_ref[2:3, :] * cu
    for i in range(CONV_K - 1):
        shift = CONV_K - 1 - i
        conv = conv + cw_ref[i:i + 1, :] * cu_buf[halo - shift:halo - shift + tm, :]
    cu_buf[0:halo, :] = cu[tm - halo:tm, :]
    yb = (gate_b * conv).astype(BF16)

    sa_ref[0] = jax.nn.sigmoid(_dot(h, _bf16(win_ref[:, o_ga:o_gb]))).astype(BF16)
    sig_b = jax.nn.sigmoid(_dot(h, _bf16(win_ref[:, o_gb:o_gb + d])))
    mb_ref[0] = (sig_b * _dot(yb, _bf16(wbb_ref[...]))).astype(BF16)

    qt = _dot_tn(_bf16(win_ref[:, 0:sbw]), h) * (LOG2E * SB_HEAD_DIM ** -0.5)
    qt = qt.astype(BF16)
    vt = _dot_tn(_bf16(win_ref[:, 2 * sbw:3 * sbw]), h).astype(BF16)
    for p in range(qt_ref.shape[1]):
        for j in range(qt_ref.shape[2]):
            rows, cols = slice(p * LANES, (p + 1) * LANES), slice(j * SB_TILE, (j + 1) * SB_TILE)
            qt_ref[0, p, j] = qt[rows, cols]
            vt_ref[0, p, j] = vt[rows, cols]
    k_ref[0] = _dot(h, _bf16(win_ref[:, sbw:2 * sbw])).astype(BF16)


def _mix_in(x, g, win, cw, wbb):
    b, s, d = x.shape
    tm = TOKEN_TILE
    sbw = SB_HEADS * SB_HEAD_DIM
    cwid = wbb.shape[0]
    assert win.shape[1] == 3 * sbw + 3 * cwid + 2 * d
    pairs = sbw // LANES
    tiled = jax.ShapeDtypeStruct((b, pairs, s // SB_TILE, LANES, SB_TILE), BF16)
    out_shape = (
        tiled,
        jax.ShapeDtypeStruct((b, s, sbw), BF16),
        tiled,
        jax.ShapeDtypeStruct((b, s, d), BF16),
        jax.ShapeDtypeStruct((b, s, d), BF16),
    )
    row = lambda bi, i: (bi, i, 0)
    tiled_spec = pl.BlockSpec((1, pairs, tm // SB_TILE, LANES, SB_TILE),
                              lambda bi, i: (bi, 0, i, 0, 0))
    return pl.pallas_call(
        _mix_in_kernel,
        grid=(b, s // tm),
        in_specs=[
            pl.BlockSpec((1, tm, d), row),
            _const_spec(g.shape), _const_spec(win.shape), _const_spec(cw.shape),
            _const_spec(wbb.shape),
        ],
        out_specs=(
            tiled_spec,
            pl.BlockSpec((1, tm, sbw), row),
            tiled_spec,
            pl.BlockSpec((1, tm, d), row),
            pl.BlockSpec((1, tm, d), row),
        ),
        out_shape=out_shape,
        scratch_shapes=[pltpu.VMEM((tm + SUBLANES, cwid), F32)],
        compiler_params=pltpu.CompilerParams(
            dimension_semantics=("arbitrary", "arbitrary"),
            vmem_limit_bytes=48 * MIB),
        name="mix_in",
    )(x, g, win, cw, wbb)


SB_STAGES = 5
SB_STAGE_GAP = 1
SB_SLOTS = SB_STAGE_GAP * (SB_STAGES - 1)


def _sb_attn_kernel(qt_ref, k_ref, vt_ref, tri_ref, o_ref,
                    qm_ref, c_ref, acc_ref, z_buf, p_buf, cum_buf, w_buf):
    nq, tq = qt_ref.shape[2], qt_ref.shape[4]
    tk = SB_TILE
    hd = SB_HEAD_DIM
    dummy = nq

    def prep(i, carry):
        qt = qt_ref[0, 0, i]
        zero = jnp.zeros((hd, tq), BF16)
        qm_ref[0, i, 0:hd, :] = qt[0:hd]
        qm_ref[0, i, hd:2 * hd, :] = zero
        qm_ref[1, i, 0:hd, :] = zero
        qm_ref[1, i, hd:2 * hd, :] = qt[hd:2 * hd]
        return carry

    lax.fori_loop(0, nq, prep, 0)
    qm_ref[:, dummy] = jnp.zeros((2, 2 * hd, tq), BF16)
    @pl.when(jnp.logical_and(pl.program_id(0) == 0, pl.program_id(1) == 0))
    def _():
        z_buf[...] = jnp.zeros(z_buf.shape, F32)
        p_buf[...] = jnp.zeros(p_buf.shape, BF16)
        cum_buf[...] = jnp.zeros(cum_buf.shape, F32)
        w_buf[...] = jnp.zeros(w_buf.shape, BF16)

    key = lax.broadcasted_iota(jnp.int32, (tk, tq), 0)
    qry = lax.broadcasted_iota(jnp.int32, (tk, tq), 1)
    causal = key < qry

    def stage_ma(i, j, h, slot):
        kt = k_ref[0, pl.ds(pl.multiple_of(j * tk, tk), tk), :]
        z_buf[slot] = _dot(kt, qm_ref[h, i])

    def softplus2(z):
        p = jnp.maximum(z, 0.0) + jnp.log(1.0 + jnp.exp2(-jnp.abs(z))) * LOG2E
        return p.astype(BF16)

    def stage_ea(slot, masked):
        z = z_buf[slot]
        if masked:
            z = jnp.where(causal, z, MASKED)
            z_buf[slot] = z
        p_buf[slot] = softplus2(z)

    def stage_mb(slot):
        cum_buf[slot] = _dot(tri_ref[...], p_buf[slot])

    def stage_eb(i, h, slot, masked):
        cum = cum_buf[slot]
        if masked:
            w_buf[slot] = jnp.exp2(z_buf[slot] - cum).astype(BF16)
            c_ref[h, i] = jnp.broadcast_to(-cum[0:1, :], (SUBLANES, tq))
        else:
            c = c_ref[h, i, 0:1, :]
            w_buf[slot] = jnp.exp2(z_buf[slot] - cum + c).astype(BF16)
            c_ref[h, i] = jnp.broadcast_to(c - cum[0:1, :], (SUBLANES, tq))

    def stage_mc(i, j, h, slot, masked):
        vt = vt_ref[0, 0, j, h * hd:(h + 1) * hd, :]
        out = _dot(vt, w_buf[slot])
        if masked:
            acc_ref[h, i] = out
        else:
            acc_ref[h, i] += out

    def run_distance(d, masked):
        n_items = 2 * (nq - d)

        def item(m):
            valid = jnp.logical_and(m >= 0, m < n_items)
            t = m // 2
            return jnp.where(valid, d + t, dummy), jnp.where(valid, t, 0)

        def body(u, carry):
            for r in range(SB_SLOTS):
                s = SB_SLOTS * u + r

                def in_stage(k):
                    back = k * SB_STAGE_GAP
                    return item(s - back), (r - back) % 2, (r - back) % SB_SLOTS

                (i4, j4), h4, slot4 = in_stage(4)
                stage_mc(i4, j4, h4, slot4, masked)
                (i3, _), h3, slot3 = in_stage(3)
                stage_eb(i3, h3, slot3, masked)
                stage_mb(in_stage(2)[2])
                stage_ea(in_stage(1)[2], masked)
                (i0, j0), h0, slot0 = in_stage(0)
                stage_ma(i0, j0, h0, slot0)
            return carry

        steps = n_items + SB_STAGE_GAP * (SB_STAGES - 1)
        lax.fori_loop(0, (steps + SB_SLOTS - 1) // SB_SLOTS, body, 0)

    def alive_after(d):
        cc = jnp.maximum(c_ref[0, 0:nq], c_ref[1, 0:nq])
        blk = lax.broadcasted_iota(jnp.int32, cc.shape, 0)
        top = jnp.max(jnp.where(blk > d, cc, MASKED), axis=0)
        return (jnp.max(top) > DEAD_LOG2).astype(jnp.int32)

    run_distance(0, True)

    def more(state):
        d, alive = state
        return jnp.logical_and(d < nq, alive > 0)

    def step(state):
        d, _ = state
        run_distance(d, False)
        return d + 1, alive_after(d)

    lax.while_loop(more, step, (jnp.int32(1), alive_after(0)))

    def emit(i, carry):
        both = jnp.concatenate([acc_ref[0, i], acc_ref[1, i]], axis=0)
        o_ref[0, pl.ds(pl.multiple_of(i * tq, tq), tq), :] = both.T.astype(BF16)
        return carry

    lax.fori_loop(0, nq, emit, 0, unroll=4)


def _sb_attn(qt, k, vt, tri):
    b, pairs, nq, _, tq = qt.shape
    s, sbw = k.shape[1], k.shape[2]
    tk = SB_TILE
    return pl.pallas_call(
        _sb_attn_kernel,
        grid=(b, pairs),
        in_specs=[
            pl.BlockSpec((1, 1, nq, LANES, tq), lambda bi, p: (bi, p, 0, 0, 0)),
            pl.BlockSpec((1, s, LANES), lambda bi, p: (bi, 0, p)),
            pl.BlockSpec((1, 1, s // tk, LANES, tk), lambda bi, p: (bi, p, 0, 0, 0)),
            _const_spec(tri.shape),
        ],
        out_specs=pl.BlockSpec((1, s, LANES), lambda bi, p: (bi, 0, p)),
        out_shape=jax.ShapeDtypeStruct((b, s, sbw), BF16),
        scratch_shapes=[
            pltpu.VMEM((2, nq + 1, LANES, tq), BF16),
            pltpu.VMEM((2, nq + 1, SUBLANES, tq), F32),
            pltpu.VMEM((2, nq + 1, SB_HEAD_DIM, tq), F32),
            pltpu.VMEM((SB_SLOTS, tk, tq), F32),
            pltpu.VMEM((SB_SLOTS, tk, tq), BF16),
            pltpu.VMEM((SB_SLOTS, tk, tq), F32),
            pltpu.VMEM((SB_SLOTS, tk, tq), BF16),
        ],
        compiler_params=pltpu.CompilerParams(
            dimension_semantics=("arbitrary", "arbitrary"),
            vmem_limit_bytes=32 * MIB),
        name="sb_attn",
    )(qt, k, vt, tri)


def _mem_kv_kernel(mem_ref, g_ref, wkv_ref, kt_ref, v_ref):
    d = mem_ref.shape[2]
    hm = _rms(mem_ref[0], g_ref[...]).astype(BF16)
    kt = _dot_tn(_bf16(wkv_ref[:, 0:d]), hm).astype(BF16)
    dh = kt_ref.shape[2]
    for hd in range(kt_ref.shape[1]):
        kt_ref[0, hd] = kt[hd * dh:(hd + 1) * dh, :]
    v_ref[0] = _dot(hm, _bf16(wkv_ref[:, d:2 * d])).astype(BF16)


def _mem_kv(mem, g, wkv):
    b, m, d = mem.shape
    dh = d // MEM_HEADS
    return pl.pallas_call(
        _mem_kv_kernel,
        grid=(b,),
        in_specs=[pl.BlockSpec((1, m, d), lambda bi: (bi, 0, 0)),
                  _const_spec(g.shape), _const_spec(wkv.shape)],
        out_specs=(pl.BlockSpec((1, MEM_HEADS, dh, m), lambda bi: (bi, 0, 0, 0)),
                   pl.BlockSpec((1, m, d), lambda bi: (bi, 0, 0))),
        out_shape=(jax.ShapeDtypeStruct((b, MEM_HEADS, dh, m), BF16),
                   jax.ShapeDtypeStruct((b, m, d), BF16)),
        compiler_params=pltpu.CompilerParams(
            dimension_semantics=("parallel",), vmem_limit_bytes=32 * MIB),
        name="mem_kv",
    )(mem, g, wkv)


def _mix_out_kernel(x_ref, oa_ref, sa_ref, mb_ref, wba_ref, wmo_ref, gq_ref, wmq_ref,
                    kmt_ref, vm_ref, wmemo_ref, o_ref):
    branch_a = _dot(oa_ref[0], _bf16(wba_ref[...]))
    merged = (sa_ref[0].astype(F32) * branch_a + mb_ref[0].astype(F32)).astype(BF16)
    x1 = x_ref[0] + _dot(merged, _bf16(wmo_ref[...]))

    dh = kmt_ref.shape[2]
    hq = _rms(x1, gq_ref[...]).astype(BF16)
    qm = (_dot(hq, _bf16(wmq_ref[...])) * (dh ** -0.5)).astype(BF16)
    cols = [slice(hd * dh, (hd + 1) * dh) for hd in range(MEM_HEADS)]
    scores = [_dot(qm[:, c], kmt_ref[0, hd]) for hd, c in enumerate(cols)]
    heads = []
    for sc, c in zip(scores, cols):
        e = jnp.exp(sc - jnp.max(sc, axis=-1, keepdims=True))
        denom = jnp.sum(e, axis=-1, keepdims=True)
        o = _dot(e.astype(BF16), vm_ref[0, :, c]) / denom
        heads.append(o.astype(BF16))
    o_ref[0] = x1 + _dot(jnp.concatenate(heads, axis=1), _bf16(wmemo_ref[...]))


def _mix_out(x, oa, sa, mb, wba, wmo, gq, wmq, kmt, vm, wmemo):
    b, s, d = x.shape
    tm = TOKEN_TILE
    row = lambda bi, i: (bi, i, 0)
    return pl.pallas_call(
        _mix_out_kernel,
        grid=(b, s // tm),
        in_specs=[
            pl.BlockSpec((1, tm, d), row),
            pl.BlockSpec((1, tm, oa.shape[2]), row),
            pl.BlockSpec((1, tm, d), row),
            pl.BlockSpec((1, tm, d), row),
            _const_spec(wba.shape), _const_spec(wmo.shape), _const_spec(gq.shape),
            _const_spec(wmq.shape),
            pl.BlockSpec((1,) + kmt.shape[1:], lambda bi, i: (bi, 0, 0, 0)),
            pl.BlockSpec((1,) + vm.shape[1:], lambda bi, i: (bi, 0, 0)),
            _const_spec(wmemo.shape),
        ],
        out_specs=pl.BlockSpec((1, tm, d), row),
        out_shape=jax.ShapeDtypeStruct((b, s, d), F32),
        compiler_params=pltpu.CompilerParams(
            dimension_semantics=("parallel", "parallel"), vmem_limit_bytes=48 * MIB),
        name="mix_out",
    )(x, oa, sa, mb, wba, wmo, gq, wmq, kmt, vm, wmemo)


def _ffn_kernel(x_ref, gf_ref, win_ref, wo_ref, gfin_ref, o_ref, *, chunks, final_norm):
    hidden = wo_ref.shape[0]
    tiles = pl.cdiv(hidden, MXU_DIM)
    bounds = [min(hidden, (c * tiles // chunks) * MXU_DIM) for c in range(chunks + 1)]
    x = x_ref[0]
    h = _rms(x, gf_ref[...]).astype(BF16)
    acts = []
    for c in range(chunks):
        lo, hi = bounds[c], bounds[c + 1]
        gate = _dot(h, _bf16(win_ref[:, lo:hi]))
        up = _dot(h, _bf16(win_ref[:, hidden + lo:hidden + hi]))
        acts.append((gate * jax.nn.sigmoid(gate) * up).astype(BF16))
    y = x
    for c in range(chunks):
        y = y + _dot(acts[c], _bf16(wo_ref[bounds[c]:bounds[c + 1], :]))
    o_ref[0] = _rms(y, gfin_ref[...]) if final_norm else y


def _ffn(x, gf, win, wo, gfin, final_norm):
    b, s, d = x.shape
    tm = TOKEN_TILE
    row = lambda bi, i: (bi, i, 0)
    chunks = 2
    assert win.shape[1] == 2 * wo.shape[0] and wo.shape[0] % LANES == 0
    return pl.pallas_call(
        functools.partial(_ffn_kernel, chunks=chunks, final_norm=final_norm),
        grid=(b, s // tm),
        in_specs=[pl.BlockSpec((1, tm, d), row), _const_spec(gf.shape), _const_spec(win.shape),
                  _const_spec(wo.shape), _const_spec(gfin.shape)],
        out_specs=pl.BlockSpec((1, tm, d), row),
        out_shape=jax.ShapeDtypeStruct((b, s, d), F32),
        compiler_params=pltpu.CompilerParams(
            dimension_semantics=("parallel", "parallel"), vmem_limit_bytes=56 * MIB),
        name="ffn",
    )(x, gf, win, wo, gfin)


def kernel(x, mem, norm_mix, w_in, conv_w, w_branch_a, w_branch_b, w_mix_out, norm_mem_q,
           norm_mem_kv, w_mem_q, w_mem_kv, w_mem_o, norm_ffn, w_ffn_in, w_ffn_out, norm_final):
    depth = w_in.shape[0]
    assert x.shape[1] % TOKEN_TILE == 0 and TOKEN_TILE % SB_TILE == 0
    assert LANES == 2 * SB_HEAD_DIM

    idx = jnp.arange(SB_TILE)
    tri = (idx[None, :] >= idx[:, None]).astype(BF16)

    for l in range(depth):
        qt, k, vt, sa, mb = _mix_in(x, norm_mix[l][None, :], w_in[l], conv_w[l], w_branch_b[l])
        oa = _sb_attn(qt, k, vt, tri)
        kmt, vm = _mem_kv(mem, norm_mem_kv[l][None, :], w_mem_kv[l])
        x = _mix_out(x, oa, sa, mb, w_branch_a[l], w_mix_out[l], norm_mem_q[l][None, :],
                     w_mem_q[l], kmt, vm, w_mem_o[l])
        x = _ffn(x, norm_ffn[l][None, :], w_ffn_in[l], w_ffn_out[l],
                 norm_final[None, :], final_norm=(l == depth - 1))
    return x
```

```python
import functools
import math

import jax
import jax.numpy as jnp
from jax import lax
from jax.experimental import pallas as pl
from jax.experimental.pallas import tpu as pltpu

EPS = 1e-6
SB_HEADS = 8
SB_HEAD_DIM = 64
CONV_K = 3
MEM_HEADS = 4

LANES = 128
SUBLANES = 8
MXU_DIM = 256
TOKEN_TILE = 512
SB_TILE = 256
MIB = 1024 * 1024

LOG2E = math.log2(math.e)
MASKED = -1e30
DEAD_LOG2 = -160.0

BF16 = jnp.bfloat16
F32 = jnp.float32


def _dot(a, b):
    return jnp.dot(a, b, preferred_element_type=F32)


def _dot_tn(a, b):
    return lax.dot_general(a, b, (((0,), (1,)), ((), ())), preferred_element_type=F32)


def _bf16(w):
    return w.astype(BF16)


def _rms_scale(x):
    return lax.rsqrt(jnp.mean(x * x, axis=-1, keepdims=True) + EPS)


def _rms(x, g):
    return x * _rms_scale(x) * g


def _const_spec(shape):
    zeros = (0,) * len(shape)
    return pl.BlockSpec(shape, lambda *_: zeros, pipeline_mode=pl.Buffered(1))


def _mix_in_kernel(x_ref, g_ref, win_ref, cw_ref, wbb_ref,
                   qt_ref, k_ref, vt_ref, sa_ref, mb_ref, cu_buf):
    tm = x_ref.shape[1]
    d = x_ref.shape[2]
    cw = wbb_ref.shape[0]
    sbw = k_ref.shape[2]
    halo = SUBLANES
    o_conv = 3 * sbw
    o_ga = o_conv + 3 * cw
    o_gb = o_ga + d

    @pl.when(pl.program_id(1) == 0)
    def _():
        cu_buf[0:halo, :] = jnp.zeros((halo, cw), F32)

    h = _rms(x_ref[0], g_ref[...]).astype(BF16)

    ugc = _dot(h, _bf16(win_ref[:, o_conv:o_ga]))
    cu = ugc[:, 2 * cw:3 * cw] * ugc[:, 0:cw]
    gate_b = ugc[:, cw:2 * cw]
    cu_buf[halo:halo + tm, :] = cu
    conv = cw_ref[2:3, :] * cu
    for i in range(CONV_K - 1):
        shift = CONV_K - 1 - i
        conv = conv + cw_ref[i:i + 1, :] * cu_buf[halo - shift:halo - shift + tm, :]
    cu_buf[0:halo, :] = cu[tm - halo:tm, :]
    yb = (gate_b * conv).astype(BF16)

    sa_ref[0] = jax.nn.sigmoid(_dot(h, _bf16(win_ref[:, o_ga:o_gb]))).astype(BF16)
    sig_b = jax.nn.sigmoid(_dot(h, _bf16(win_ref[:, o_gb:o_gb + d])))
    mb_ref[0] = (sig_b * _dot(yb, _bf16(wbb_ref[...]))).astype(BF16)

    qt = _dot_tn(_bf16(win_ref[:, 0:sbw]), h) * (LOG2E * SB_HEAD_DIM ** -0.5)
    qt = qt.astype(BF16)
    vt = _dot_tn(_bf16(win_ref[:, 2 * sbw:3 * sbw]), h).astype(BF16)
    for p in range(qt_ref.shape[1]):
        for j in range(qt_ref.shape[2]):
            rows, cols = slice(p * LANES, (p + 1) * LANES), slice(j * SB_TILE, (j + 1) * SB_TILE)
            qt_ref[0, p, j] = qt[rows, cols]
            vt_ref[0, p, j] = vt[rows, cols]
    k_ref[0] = _dot(h, _bf16(win_ref[:, sbw:2 * sbw])).astype(BF16)


def _mix_in(x, g, win, cw, wbb):
    b, s, d = x.shape
    tm = TOKEN_TILE
    sbw = SB_HEADS * SB_HEAD_DIM
    cwid = wbb.shape[0]
    assert win.shape[1] == 3 * sbw + 3 * cwid + 2 * d
    pairs = sbw // LANES
    tiled = jax.ShapeDtypeStruct((b, pairs, s // SB_TILE, LANES, SB_TILE), BF16)
    out_shape = (
        tiled,
        jax.ShapeDtypeStruct((b, s, sbw), BF16),
        tiled,
        jax.ShapeDtypeStruct((b, s, d), BF16),
        jax.ShapeDtypeStruct((b, s, d), BF16),
    )
    row = lambda bi, i: (bi, i, 0)
    tiled_spec = pl.BlockSpec((1, pairs, tm // SB_TILE, LANES, SB_TILE),
                              lambda bi, i: (bi, 0, i, 0, 0))
    return pl.pallas_call(
        _mix_in_kernel,
        grid=(b, s // tm),
        in_specs=[
            pl.BlockSpec((1, tm, d), row),
            _const_spec(g.shape), _const_spec(win.shape), _const_spec(cw.shape),
            _const_spec(wbb.shape),
        ],
        out_specs=(
            tiled_spec,
            pl.BlockSpec((1, tm, sbw), row),
            tiled_spec,
            pl.BlockSpec((1, tm, d), row),
            pl.BlockSpec((1, tm, d), row),
        ),
        out_shape=out_shape,
        scratch_shapes=[pltpu.VMEM((tm + SUBLANES, cwid), F32)],
        compiler_params=pltpu.CompilerParams(
            dimension_semantics=("arbitrary", "arbitrary"),
            vmem_limit_bytes=48 * MIB),
        name="mix_in",
    )(x, g, win, cw, wbb)


SB_STAGES = 5
SB_STAGE_GAP = 1
SB_SLOTS = SB_STAGE_GAP * (SB_STAGES - 1)


def _sb_attn_kernel(qt_ref, k_ref, vt_ref, tri_ref, o_ref,
                    qm_ref, c_ref, acc_ref, z_buf, p_buf, cum_buf, w_buf):
    nq, tq = qt_ref.shape[2], qt_ref.shape[4]
    tk = SB_TILE
    hd = SB_HEAD_DIM
    dummy = nq

    def prep(i, carry):
        qt = qt_ref[0, 0, i]
        zero = jnp.zeros((hd, tq), BF16)
        qm_ref[0, i, 0:hd, :] = qt[0:hd]
        qm_ref[0, i, hd:2 * hd, :] = zero
        qm_ref[1, i, 0:hd, :] = zero
        qm_ref[1, i, hd:2 * hd, :] = qt[hd:2 * hd]
        return carry

    lax.fori_loop(0, nq, prep, 0)
    qm_ref[:, dummy] = jnp.zeros((2, 2 * hd, tq), BF16)
    @pl.when(jnp.logical_and(pl.program_id(0) == 0, pl.program_id(1) == 0))
    def _():
        z_buf[...] = jnp.zeros(z_buf.shape, F32)
        p_buf[...] = jnp.zeros(p_buf.shape, BF16)
        cum_buf[...] = jnp.zeros(cum_buf.shape, F32)
        w_buf[...] = jnp.zeros(w_buf.shape, BF16)

    key = lax.broadcasted_iota(jnp.int32, (tk, tq), 0)
    qry = lax.broadcasted_iota(jnp.int32, (tk, tq), 1)
    causal = key < qry

    def stage_ma(i, j, h, slot):
        kt = k_ref[0, pl.ds(pl.multiple_of(j * tk, tk), tk), :]
        z_buf[slot] = _dot(kt, qm_ref[h, i])

    def softplus2(z):
        p = jnp.maximum(z, 0.0) + jnp.log(1.0 + jnp.exp2(-jnp.abs(z))) * LOG2E
        return p.astype(BF16)

    def stage_ea(slot, masked):
        z = z_buf[slot]
        if masked:
            z = jnp.where(causal, z, MASKED)
            z_buf[slot] = z
        p_buf[slot] = softplus2(z)

    def stage_mb(slot):
        cum_buf[slot] = _dot(tri_ref[...], p_buf[slot])

    def stage_eb(i, h, slot, masked):
        cum = cum_buf[slot]
        if masked:
            w_buf[slot] = jnp.exp2(z_buf[slot] - cum).astype(BF16)
            c_ref[h, i] = jnp.broadcast_to(-cum[0:1, :], (SUBLANES, tq))
        else:
            c = c_ref[h, i, 0:1, :]
            w_buf[slot] = jnp.exp2(z_buf[slot] - cum + c).astype(BF16)
            c_ref[h, i] = jnp.broadcast_to(c - cum[0:1, :], (SUBLANES, tq))

    def stage_mc(i, j, h, slot, masked):
        vt = vt_ref[0, 0, j, h * hd:(h + 1) * hd, :]
        out = _dot(vt, w_buf[slot])
        if masked:
            acc_ref[h, i] = out
        else:
            acc_ref[h, i] += out

    def block(item, diagonal):
        def body(u, carry):
            for r in range(SB_SLOTS):
                s = SB_SLOTS * u + r

                def in_stage(k):
                    back = k * SB_STAGE_GAP
                    return item(s - back), (r - back) % 2, (r - back) % SB_SLOTS

                (i4, j4), h4, slot4 = in_stage(4)
                stage_mc(i4, j4, h4, slot4, diagonal(4, r))
                (i3, _), h3, slot3 = in_stage(3)
                stage_eb(i3, h3, slot3, diagonal(3, r))
                stage_mb(in_stage(2)[2])
                stage_ea(in_stage(1)[2], diagonal(1, r))
                (i0, j0), h0, slot0 = in_stage(0)
                stage_ma(i0, j0, h0, slot0)
            return carry
        return body

    drain = SB_STAGE_GAP * (SB_STAGES - 1)

    def blocks_for(steps):
        return (steps + SB_SLOTS - 1) // SB_SLOTS

    def run_head():
        n_diag, n_next = 2 * nq, 2 * (nq - 1)

        def item(m):
            t0, t1 = m // 2, (m - n_diag) // 2
            in_diag = jnp.logical_and(m >= 0, m < n_diag)
            in_next = jnp.logical_and(m >= n_diag, m < n_diag + n_next)
            i = jnp.where(in_diag, t0, jnp.where(in_next, t1 + 1, dummy))
            j = jnp.where(in_diag, t0, jnp.where(in_next, t1, 0))
            return i, j

        turn = n_diag // SB_SLOTS
        lax.fori_loop(0, turn, block(item, lambda k, r: True), 0)
        block(item, lambda k, r: r < k * SB_STAGE_GAP)(turn, 0)
        lax.fori_loop(turn + 1, blocks_for(n_diag + n_next + drain),
                      block(item, lambda k, r: False), 0)

    def run_distance(d):
        n_items = 2 * (nq - d)

        def item(m):
            valid = jnp.logical_and(m >= 0, m < n_items)
            t = m // 2
            return jnp.where(valid, d + t, dummy), jnp.where(valid, t, 0)

        lax.fori_loop(0, blocks_for(n_items + drain), block(item, lambda k, r: False), 0)

    def alive_after(d):
        cc = jnp.maximum(c_ref[0, 0:nq], c_ref[1, 0:nq])
        blk = lax.broadcasted_iota(jnp.int32, cc.shape, 0)
        top = jnp.max(jnp.where(blk > d, cc, MASKED), axis=0)
        return (jnp.max(top) > DEAD_LOG2).astype(jnp.int32)

    run_head()

    def more(state):
        d, alive = state
        return jnp.logical_and(d < nq, alive > 0)

    def step(state):
        d, _ = state
        run_distance(d)
        return d + 1, alive_after(d)

    lax.while_loop(more, step, (jnp.int32(2), alive_after(1)))

    def emit(i, carry):
        both = jnp.concatenate([acc_ref[0, i], acc_ref[1, i]], axis=0)
        o_ref[0, pl.ds(pl.multiple_of(i * tq, tq), tq), :] = both.T.astype(BF16)
        return carry

    lax.fori_loop(0, nq, emit, 0, unroll=4)


def _sb_attn(qt, k, vt, tri):
    b, pairs, nq, _, tq = qt.shape
    s, sbw = k.shape[1], k.shape[2]
    tk = SB_TILE
    assert nq >= 2 and (2 * nq) % SB_SLOTS == 0 and 2 * nq - 2 > SB_STAGE_GAP * (SB_STAGES - 1)
    return pl.pallas_call(
        _sb_attn_kernel,
        grid=(b, pairs),
        in_specs=[
            pl.BlockSpec((1, 1, nq, LANES, tq), lambda bi, p: (bi, p, 0, 0, 0)),
            pl.BlockSpec((1, s, LANES), lambda bi, p: (bi, 0, p)),
            pl.BlockSpec((1, 1, s // tk, LANES, tk), lambda bi, p: (bi, p, 0, 0, 0)),
            _const_spec(tri.shape),
        ],
        out_specs=pl.BlockSpec((1, s, LANES), lambda bi, p: (bi, 0, p)),
        out_shape=jax.ShapeDtypeStruct((b, s, sbw), BF16),
        scratch_shapes=[
            pltpu.VMEM((2, nq + 1, LANES, tq), BF16),
            pltpu.VMEM((2, nq + 1, SUBLANES, tq), F32),
            pltpu.VMEM((2, nq + 1, SB_HEAD_DIM, tq), F32),
            pltpu.VMEM((SB_SLOTS, tk, tq), F32),
            pltpu.VMEM((SB_SLOTS, tk, tq), BF16),
            pltpu.VMEM((SB_SLOTS, tk, tq), F32),
            pltpu.VMEM((SB_SLOTS, tk, tq), BF16),
        ],
        compiler_params=pltpu.CompilerParams(
            dimension_semantics=("arbitrary", "arbitrary"),
            vmem_limit_bytes=32 * MIB),
        name="sb_attn",
    )(qt, k, vt, tri)


def _mem_kv_kernel(mem_ref, g_ref, wkv_ref, kt_ref, v_ref):
    d = mem_ref.shape[2]
    hm = _rms(mem_ref[0], g_ref[...]).astype(BF16)
    kt = _dot_tn(_bf16(wkv_ref[:, 0:d]), hm).astype(BF16)
    dh = kt_ref.shape[2]
    for hd in range(kt_ref.shape[1]):
        kt_ref[0, hd] = kt[hd * dh:(hd + 1) * dh, :]
    v_ref[0] = _dot(hm, _bf16(wkv_ref[:, d:2 * d])).astype(BF16)


def _mem_kv(mem, g, wkv):
    b, m, d = mem.shape
    dh = d // MEM_HEADS
    return pl.pallas_call(
        _mem_kv_kernel,
        grid=(b,),
        in_specs=[pl.BlockSpec((1, m, d), lambda bi: (bi, 0, 0)),
                  _const_spec(g.shape), _const_spec(wkv.shape)],
        out_specs=(pl.BlockSpec((1, MEM_HEADS, dh, m), lambda bi: (bi, 0, 0, 0)),
                   pl.BlockSpec((1, m, d), lambda bi: (bi, 0, 0))),
        out_shape=(jax.ShapeDtypeStruct((b, MEM_HEADS, dh, m), BF16),
                   jax.ShapeDtypeStruct((b, m, d), BF16)),
        compiler_params=pltpu.CompilerParams(
            dimension_semantics=("parallel",), vmem_limit_bytes=32 * MIB),
        name="mem_kv",
    )(mem, g, wkv)


def _mix_out_kernel(x_ref, oa_ref, sa_ref, mb_ref, wba_ref, wmo_ref, gq_ref, wmq_ref,
                    kmt_ref, vm_ref, wmemo_ref, o_ref):
    branch_a = _dot(oa_ref[0], _bf16(wba_ref[...]))
    merged = (sa_ref[0].astype(F32) * branch_a + mb_ref[0].astype(F32)).astype(BF16)
    x1 = x_ref[0] + _dot(merged, _bf16(wmo_ref[...]))

    dh = kmt_ref.shape[2]
    xg = (x1 * gq_ref[...]).astype(BF16)
    qm = (_dot(xg, _bf16(wmq_ref[...])) * (_rms_scale(x1) * (dh ** -0.5))).astype(BF16)
    cols = [slice(hd * dh, (hd + 1) * dh) for hd in range(MEM_HEADS)]
    scores = [_dot(qm[:, c], kmt_ref[0, hd]) for hd, c in enumerate(cols)]
    heads = []
    for sc, c in zip(scores, cols):
        e = jnp.exp(sc - jnp.max(sc, axis=-1, keepdims=True))
        denom = jnp.sum(e, axis=-1, keepdims=True)
        o = _dot(e.astype(BF16), vm_ref[0, :, c]) / denom
        heads.append(o.astype(BF16))
    o_ref[0] = x1 + _dot(jnp.concatenate(heads, axis=1), _bf16(wmemo_ref[...]))


def _mix_out(x, oa, sa, mb, wba, wmo, gq, wmq, kmt, vm, wmemo):
    b, s, d = x.shape
    tm = TOKEN_TILE
    row = lambda bi, i: (bi, i, 0)
    return pl.pallas_call(
        _mix_out_kernel,
        grid=(b, s // tm),
        in_specs=[
            pl.BlockSpec((1, tm, d), row),
            pl.BlockSpec((1, tm, oa.shape[2]), row),
            pl.BlockSpec((1, tm, d), row),
            pl.BlockSpec((1, tm, d), row),
            _const_spec(wba.shape), _const_spec(wmo.shape), _const_spec(gq.shape),
            _const_spec(wmq.shape),
            pl.BlockSpec((1,) + kmt.shape[1:], lambda bi, i: (bi, 0, 0, 0)),
            pl.BlockSpec((1,) + vm.shape[1:], lambda bi, i: (bi, 0, 0)),
            _const_spec(wmemo.shape),
        ],
        out_specs=pl.BlockSpec((1, tm, d), row),
        out_shape=jax.ShapeDtypeStruct((b, s, d), F32),
        compiler_params=pltpu.CompilerParams(
            dimension_semantics=("parallel", "parallel"), vmem_limit_bytes=48 * MIB),
        name="mix_out",
    )(x, oa, sa, mb, wba, wmo, gq, wmq, kmt, vm, wmemo)


def _ffn_kernel(x_ref, gf_ref, win_ref, wo_ref, gfin_ref, o_ref, *, chunks, final_norm):
    hidden = wo_ref.shape[0]
    tiles = pl.cdiv(hidden, MXU_DIM)
    bounds = [min(hidden, (c * tiles // chunks) * MXU_DIM) for c in range(chunks + 1)]
    x = x_ref[0]
    xg = (x * gf_ref[...]).astype(BF16)
    r = _rms_scale(x)
    acts = []
    for c in range(chunks):
        lo, hi = bounds[c], bounds[c + 1]
        gate = _dot(xg, _bf16(win_ref[:, lo:hi])) * r
        up = _dot(xg, _bf16(win_ref[:, hidden + lo:hidden + hi])) * r
        acts.append((gate * jax.nn.sigmoid(gate) * up).astype(BF16))
    y = x
    for c in range(chunks):
        y = y + _dot(acts[c], _bf16(wo_ref[bounds[c]:bounds[c + 1], :]))
    o_ref[0] = _rms(y, gfin_ref[...]) if final_norm else y


def _ffn(x, gf, win, wo, gfin, final_norm):
    b, s, d = x.shape
    tm = TOKEN_TILE
    row = lambda bi, i: (bi, i, 0)
    chunks = 2
    assert win.shape[1] == 2 * wo.shape[0] and wo.shape[0] % LANES == 0
    return pl.pallas_call(
        functools.partial(_ffn_kernel, chunks=chunks, final_norm=final_norm),
        grid=(b, s // tm),
        in_specs=[pl.BlockSpec((1, tm, d), row), _const_spec(gf.shape), _const_spec(win.shape),
                  _const_spec(wo.shape), _const_spec(gfin.shape)],
        out_specs=pl.BlockSpec((1, tm, d), row),
        out_shape=jax.ShapeDtypeStruct((b, s, d), F32),
        compiler_params=pltpu.CompilerParams(
            dimension_semantics=("parallel", "parallel"), vmem_limit_bytes=56 * MIB),
        name="ffn",
    )(x, gf, win, wo, gfin)


def kernel(x, mem, norm_mix, w_in, conv_w, w_branch_a, w_branch_b, w_mix_out, norm_mem_q,
           norm_mem_kv, w_mem_q, w_mem_kv, w_mem_o, norm_ffn, w_ffn_in, w_ffn_out, norm_final):
    depth = w_in.shape[0]
    assert x.shape[1] % TOKEN_TILE == 0 and TOKEN_TILE % SB_TILE == 0
    assert LANES == 2 * SB_HEAD_DIM

    idx = jnp.arange(SB_TILE)
    tri = (idx[None, :] >= idx[:, None]).astype(BF16)

    for l in range(depth):
        qt, k, vt, sa, mb = _mix_in(x, norm_mix[l][None, :], w_in[l], conv_w[l], w_branch_b[l])
        oa = _sb_attn(qt, k, vt, tri)
        kmt, vm = _mem_kv(mem, norm_mem_kv[l][None, :], w_mem_kv[l])
        x = _mix_out(x, oa, sa, mb, w_branch_a[l], w_mix_out[l], norm_mem_q[l][None, :],
                     w_mem_q[l], kmt, vm, w_mem_o[l])
        x = _ffn(x, norm_ffn[l][None, :], w_ffn_in[l], w_ffn_out[l],
                 norm_final[None, :], final_norm=(l == depth - 1))
    return x
```

```python
import functools
import math

import jax
import jax.numpy as jnp
from jax import lax
from jax.experimental import pallas as pl
from jax.experimental.pallas import tpu as pltpu

EPS = 1e-6
SB_HEADS = 8
SB_HEAD_DIM = 64
CONV_K = 3
MEM_HEADS = 4

LANES = 128
SUBLANES = 8
MXU_DIM = 256
TOKEN_TILE = 512
SB_TILE = 256
MIB = 1024 * 1024
VMEM_LIMIT = 56 * MIB

LOG2E = math.log2(math.e)
MASKED = -1e30
DEAD_LOG2 = -160.0

BF16 = jnp.bfloat16
F32 = jnp.float32


def _dot(a, b):
    return jnp.dot(a, b, preferred_element_type=F32)


def _dot_tn(a, b):
    return lax.dot_general(a, b, (((0,), (1,)), ((), ())), preferred_element_type=F32)


def _bf16(w):
    return w.astype(BF16)


def _rms_scale(x):
    return lax.rsqrt(jnp.mean(x * x, axis=-1, keepdims=True) + EPS)


def _rms(x, g):
    return x * _rms_scale(x) * g


def _const_spec(shape):
    zeros = (0,) * len(shape)
    return pl.BlockSpec(shape, lambda *_: zeros, pipeline_mode=pl.Buffered(1))


def _mix_in_kernel(x_ref, g_ref, win_ref, cw_ref, wbb_ref,
                   qt_ref, k_ref, vt_ref, sa_ref, mb_ref, cu_buf):
    tm = x_ref.shape[1]
    d = x_ref.shape[2]
    cw = wbb_ref.shape[0]
    sbw = k_ref.shape[2]
    halo = SUBLANES
    o_conv = 3 * sbw
    o_ga = o_conv + 3 * cw
    o_gb = o_ga + d

    @pl.when(pl.program_id(1) == 0)
    def _():
        cu_buf[0:halo, :] = jnp.zeros((halo, cw), F32)

    h = _rms(x_ref[0], g_ref[...]).astype(BF16)

    ugc = _dot(h, _bf16(win_ref[:, o_conv:o_ga]))
    cu = ugc[:, 2 * cw:3 * cw] * ugc[:, 0:cw]
    gate_b = ugc[:, cw:2 * cw]
    cu_buf[halo:halo + tm, :] = cu
    conv = cw_ref[2:3, :] * cu
    for i in range(CONV_K - 1):
        shift = CONV_K - 1 - i
        conv = conv + cw_ref[i:i + 1, :] * cu_buf[halo - shift:halo - shift + tm, :]
    cu_buf[0:halo, :] = cu[tm - halo:tm, :]
    yb = (gate_b * conv).astype(BF16)

    sa_ref[0] = jax.nn.sigmoid(_dot(h, _bf16(win_ref[:, o_ga:o_gb]))).astype(BF16)
    sig_b = jax.nn.sigmoid(_dot(h, _bf16(win_ref[:, o_gb:o_gb + d])))
    mb_ref[0] = (sig_b * _dot(yb, _bf16(wbb_ref[...]))).astype(BF16)

    qt = _dot_tn(_bf16(win_ref[:, 0:sbw]), h) * (LOG2E * SB_HEAD_DIM ** -0.5)
    qt = qt.astype(BF16)
    vt = _dot_tn(_bf16(win_ref[:, 2 * sbw:3 * sbw]), h).astype(BF16)
    for p in range(qt_ref.shape[1]):
        for j in range(qt_ref.shape[2]):
            rows, cols = slice(p * LANES, (p + 1) * LANES), slice(j * SB_TILE, (j + 1) * SB_TILE)
            qt_ref[0, p, j] = qt[rows, cols]
            vt_ref[0, p, j] = vt[rows, cols]
    k_ref[0] = _dot(h, _bf16(win_ref[:, sbw:2 * sbw])).astype(BF16)


def _mix_in(x, g, win, cw, wbb):
    b, s, d = x.shape
    tm = TOKEN_TILE
    sbw = SB_HEADS * SB_HEAD_DIM
    cwid = wbb.shape[0]
    assert win.shape[1] == 3 * sbw + 3 * cwid + 2 * d
    pairs = sbw // LANES
    tiled = jax.ShapeDtypeStruct((b, pairs, s // SB_TILE, LANES, SB_TILE), BF16)
    out_shape = (
        tiled,
        jax.ShapeDtypeStruct((b, s, sbw), BF16),
        tiled,
        jax.ShapeDtypeStruct((b, s, d), BF16),
        jax.ShapeDtypeStruct((b, s, d), BF16),
    )
    row = lambda bi, i: (bi, i, 0)
    tiled_spec = pl.BlockSpec((1, pairs, tm // SB_TILE, LANES, SB_TILE),
                              lambda bi, i: (bi, 0, i, 0, 0))
    return pl.pallas_call(
        _mix_in_kernel,
        grid=(b, s // tm),
        in_specs=[
            pl.BlockSpec((1, tm, d), row),
            _const_spec(g.shape), _const_spec(win.shape), _const_spec(cw.shape),
            _const_spec(wbb.shape),
        ],
        out_specs=(
            tiled_spec,
            pl.BlockSpec((1, tm, sbw), row),
            tiled_spec,
            pl.BlockSpec((1, tm, d), row),
            pl.BlockSpec((1, tm, d), row),
        ),
        out_shape=out_shape,
        scratch_shapes=[pltpu.VMEM((tm + SUBLANES, cwid), F32)],
        compiler_params=pltpu.CompilerParams(
            dimension_semantics=("arbitrary", "arbitrary"),
            vmem_limit_bytes=VMEM_LIMIT),
        name="mix_in",
    )(x, g, win, cw, wbb)


SB_STAGES = 5
SB_STAGE_GAP = 1
SB_SLOTS = SB_STAGE_GAP * (SB_STAGES - 1)


def _sb_attn_kernel(qt_ref, k_ref, vt_ref, tri_ref, o_ref,
                    qm_ref, c_ref, acc_ref, z_buf, p_buf, cum_buf, w_buf):
    nq, tq = qt_ref.shape[2], qt_ref.shape[4]
    tk = SB_TILE
    hd = SB_HEAD_DIM
    dummy = nq

    def prep(i, carry):
        qt = qt_ref[0, 0, i]
        zero = jnp.zeros((hd, tq), BF16)
        qm_ref[0, i, 0:hd, :] = qt[0:hd]
        qm_ref[0, i, hd:2 * hd, :] = zero
        qm_ref[1, i, 0:hd, :] = zero
        qm_ref[1, i, hd:2 * hd, :] = qt[hd:2 * hd]
        return carry

    lax.fori_loop(0, nq, prep, 0)
    qm_ref[:, dummy] = jnp.zeros((2, 2 * hd, tq), BF16)
    @pl.when(jnp.logical_and(pl.program_id(0) == 0, pl.program_id(1) == 0))
    def _():
        z_buf[...] = jnp.zeros(z_buf.shape, F32)
        p_buf[...] = jnp.zeros(p_buf.shape, BF16)
        cum_buf[...] = jnp.zeros(cum_buf.shape, F32)
        w_buf[...] = jnp.zeros(w_buf.shape, BF16)

    key = lax.broadcasted_iota(jnp.int32, (tk, tq), 0)
    qry = lax.broadcasted_iota(jnp.int32, (tk, tq), 1)
    causal = key < qry

    def stage_ma(i, j, h, slot):
        kt = k_ref[0, pl.ds(pl.multiple_of(j * tk, tk), tk), :]
        z_buf[slot] = _dot(kt, qm_ref[h, i])

    def softplus2(z):
        p = jnp.maximum(z, 0.0) + jnp.log(1.0 + jnp.exp2(-jnp.abs(z))) * LOG2E
        return p.astype(BF16)

    def stage_ea(slot, masked):
        z = z_buf[slot]
        if masked:
            z = jnp.where(causal, z, MASKED)
            z_buf[slot] = z
        p_buf[slot] = softplus2(z)

    def stage_mb(slot):
        cum_buf[slot] = _dot(tri_ref[...], p_buf[slot])

    def stage_eb(i, h, slot, masked):
        cum = cum_buf[slot]
        if masked:
            w_buf[slot] = jnp.exp2(z_buf[slot] - cum).astype(BF16)
            c_ref[h, i] = jnp.broadcast_to(-cum[0:1, :], (SUBLANES, tq))
        else:
            c = c_ref[h, i, 0:1, :]
            w_buf[slot] = jnp.exp2(z_buf[slot] - cum + c).astype(BF16)
            c_ref[h, i] = jnp.broadcast_to(c - cum[0:1, :], (SUBLANES, tq))

    def stage_mc(i, j, h, slot, masked):
        vt = vt_ref[0, 0, j, h * hd:(h + 1) * hd, :]
        out = _dot(vt, w_buf[slot])
        if masked:
            acc_ref[h, i] = out
        else:
            acc_ref[h, i] += out

    def block(item, diagonal):
        def body(u, carry):
            for r in range(SB_SLOTS):
                s = SB_SLOTS * u + r

                def in_stage(k):
                    back = k * SB_STAGE_GAP
                    return item(s - back), (r - back) % 2, (r - back) % SB_SLOTS

                (i4, j4), h4, slot4 = in_stage(4)
                stage_mc(i4, j4, h4, slot4, diagonal(4, r))
                (i3, _), h3, slot3 = in_stage(3)
                stage_eb(i3, h3, slot3, diagonal(3, r))
                stage_mb(in_stage(2)[2])
                stage_ea(in_stage(1)[2], diagonal(1, r))
                (i0, j0), h0, slot0 = in_stage(0)
                stage_ma(i0, j0, h0, slot0)
            return carry
        return body

    drain = SB_STAGE_GAP * (SB_STAGES - 1)

    def blocks_for(steps):
        return (steps + SB_SLOTS - 1) // SB_SLOTS

    def run_head():
        n_diag, n_next = 2 * nq, 2 * (nq - 1)

        def item(m):
            t0, t1 = m // 2, (m - n_diag) // 2
            in_diag = jnp.logical_and(m >= 0, m < n_diag)
            in_next = jnp.logical_and(m >= n_diag, m < n_diag + n_next)
            i = jnp.where(in_diag, t0, jnp.where(in_next, t1 + 1, dummy))
            j = jnp.where(in_diag, t0, jnp.where(in_next, t1, 0))
            return i, j

        turn = n_diag // SB_SLOTS
        lax.fori_loop(0, turn, block(item, lambda k, r: True), 0)
        block(item, lambda k, r: r < k * SB_STAGE_GAP)(turn, 0)
        lax.fori_loop(turn + 1, blocks_for(n_diag + n_next + drain),
                      block(item, lambda k, r: False), 0)

    def run_distance(d):
        n_items = 2 * (nq - d)

        def item(m):
            valid = jnp.logical_and(m >= 0, m < n_items)
            t = m // 2
            return jnp.where(valid, d + t, dummy), jnp.where(valid, t, 0)

        lax.fori_loop(0, blocks_for(n_items + drain), block(item, lambda k, r: False), 0)

    def alive_after(d):
        cc = jnp.maximum(c_ref[0, 0:nq], c_ref[1, 0:nq])
        blk = lax.broadcasted_iota(jnp.int32, cc.shape, 0)
        top = jnp.max(jnp.where(blk > d, cc, MASKED), axis=0)
        return (jnp.max(top) > DEAD_LOG2).astype(jnp.int32)

    run_head()

    def more(state):
        d, alive = state
        return jnp.logical_and(d < nq, alive > 0)

    def step(state):
        d, _ = state
        run_distance(d)
        return d + 1, alive_after(d)

    lax.while_loop(more, step, (jnp.int32(2), alive_after(1)))

    def emit(i, carry):
        both = jnp.concatenate([acc_ref[0, i], acc_ref[1, i]], axis=0)
        o_ref[0, pl.ds(pl.multiple_of(i * tq, tq), tq), :] = both.T.astype(BF16)
        return carry

    lax.fori_loop(0, nq, emit, 0, unroll=4)


def _sb_attn(qt, k, vt, tri):
    b, pairs, nq, _, tq = qt.shape
    s, sbw = k.shape[1], k.shape[2]
    tk = SB_TILE
    assert nq >= 2 and (2 * nq) % SB_SLOTS == 0 and 2 * nq - 2 > SB_STAGE_GAP * (SB_STAGES - 1)
    return pl.pallas_call(
        _sb_attn_kernel,
        grid=(b, pairs),
        in_specs=[
            pl.BlockSpec((1, 1, nq, LANES, tq), lambda bi, p: (bi, p, 0, 0, 0)),
            pl.BlockSpec((1, s, LANES), lambda bi, p: (bi, 0, p)),
            pl.BlockSpec((1, 1, s // tk, LANES, tk), lambda bi, p: (bi, p, 0, 0, 0)),
            _const_spec(tri.shape),
        ],
        out_specs=pl.BlockSpec((1, s, LANES), lambda bi, p: (bi, 0, p)),
        out_shape=jax.ShapeDtypeStruct((b, s, sbw), BF16),
        scratch_shapes=[
            pltpu.VMEM((2, nq + 1, LANES, tq), BF16),
            pltpu.VMEM((2, nq + 1, SUBLANES, tq), F32),
            pltpu.VMEM((2, nq + 1, SB_HEAD_DIM, tq), F32),
            pltpu.VMEM((SB_SLOTS, tk, tq), F32),
            pltpu.VMEM((SB_SLOTS, tk, tq), BF16),
            pltpu.VMEM((SB_SLOTS, tk, tq), F32),
            pltpu.VMEM((SB_SLOTS, tk, tq), BF16),
        ],
        compiler_params=pltpu.CompilerParams(
            dimension_semantics=("arbitrary", "arbitrary"),
            vmem_limit_bytes=VMEM_LIMIT),
        name="sb_attn",
    )(qt, k, vt, tri)


def _mem_kv_kernel(mem_ref, g_ref, wkv_ref, kt_ref, v_ref):
    d = mem_ref.shape[2]
    hm = _rms(mem_ref[0], g_ref[...]).astype(BF16)
    kt = _dot_tn(_bf16(wkv_ref[:, 0:d]), hm).astype(BF16)
    dh = kt_ref.shape[2]
    for hd in range(kt_ref.shape[1]):
        kt_ref[0, hd] = kt[hd * dh:(hd + 1) * dh, :]
    v_ref[0] = _dot(hm, _bf16(wkv_ref[:, d:2 * d])).astype(BF16)


def _mem_kv(mem, g, wkv):
    b, m, d = mem.shape
    dh = d // MEM_HEADS
    return pl.pallas_call(
        _mem_kv_kernel,
        grid=(b,),
        in_specs=[pl.BlockSpec((1, m, d), lambda bi: (bi, 0, 0)),
                  _const_spec(g.shape), _const_spec(wkv.shape)],
        out_specs=(pl.BlockSpec((1, MEM_HEADS, dh, m), lambda bi: (bi, 0, 0, 0)),
                   pl.BlockSpec((1, m, d), lambda bi: (bi, 0, 0))),
        out_shape=(jax.ShapeDtypeStruct((b, MEM_HEADS, dh, m), BF16),
                   jax.ShapeDtypeStruct((b, m, d), BF16)),
        compiler_params=pltpu.CompilerParams(
            dimension_semantics=("parallel",), vmem_limit_bytes=VMEM_LIMIT),
        name="mem_kv",
    )(mem, g, wkv)


def _mix_out_kernel(x_ref, oa_ref, sa_ref, mb_ref, wba_ref, wmo_ref, gq_ref, wmq_ref,
                    kmt_ref, vm_ref, wmemo_ref, o_ref):
    branch_a = _dot(oa_ref[0], _bf16(wba_ref[...]))
    merged = (sa_ref[0].astype(F32) * branch_a + mb_ref[0].astype(F32)).astype(BF16)
    x1 = x_ref[0] + _dot(merged, _bf16(wmo_ref[...]))

    dh = kmt_ref.shape[2]
    xg = (x1 * gq_ref[...]).astype(BF16)
    qm = (_dot(xg, _bf16(wmq_ref[...])) * (_rms_scale(x1) * (dh ** -0.5))).astype(BF16)
    cols = [slice(hd * dh, (hd + 1) * dh) for hd in range(MEM_HEADS)]
    scores = [_dot(qm[:, c], kmt_ref[0, hd]) for hd, c in enumerate(cols)]
    heads = []
    for sc, c in zip(scores, cols):
        e = jnp.exp(sc - jnp.max(sc, axis=-1, keepdims=True))
        denom = jnp.sum(e, axis=-1, keepdims=True)
        o = _dot(e.astype(BF16), vm_ref[0, :, c]) / denom
        heads.append(o.astype(BF16))
    o_ref[0] = x1 + _dot(jnp.concatenate(heads, axis=1), _bf16(wmemo_ref[...]))


def _mix_out(x, oa, sa, mb, wba, wmo, gq, wmq, kmt, vm, wmemo):
    b, s, d = x.shape
    tm = TOKEN_TILE
    row = lambda bi, i: (bi, i, 0)
    return pl.pallas_call(
        _mix_out_kernel,
        grid=(b, s // tm),
        in_specs=[
            pl.BlockSpec((1, tm, d), row),
            pl.BlockSpec((1, tm, oa.shape[2]), row),
            pl.BlockSpec((1, tm, d), row),
            pl.BlockSpec((1, tm, d), row),
            _const_spec(wba.shape), _const_spec(wmo.shape), _const_spec(gq.shape),
            _const_spec(wmq.shape),
            pl.BlockSpec((1,) + kmt.shape[1:], lambda bi, i: (bi, 0, 0, 0)),
            pl.BlockSpec((1,) + vm.shape[1:], lambda bi, i: (bi, 0, 0)),
            _const_spec(wmemo.shape),
        ],
        out_specs=pl.BlockSpec((1, tm, d), row),
        out_shape=jax.ShapeDtypeStruct((b, s, d), F32),
        compiler_params=pltpu.CompilerParams(
            dimension_semantics=("parallel", "parallel"), vmem_limit_bytes=VMEM_LIMIT),
        name="mix_out",
    )(x, oa, sa, mb, wba, wmo, gq, wmq, kmt, vm, wmemo)


def _ffn_kernel(x_ref, gf_ref, win_ref, wo_ref, gfin_ref, o_ref, *, chunks, final_norm):
    hidden = wo_ref.shape[0]
    tiles = pl.cdiv(hidden, MXU_DIM)
    bounds = [min(hidden, (c * tiles // chunks) * MXU_DIM) for c in range(chunks + 1)]
    x = x_ref[0]
    xg = (x * gf_ref[...]).astype(BF16)
    r = _rms_scale(x)
    acts = []
    for c in range(chunks):
        lo, hi = bounds[c], bounds[c + 1]
        gate = _dot(xg, _bf16(win_ref[:, lo:hi])) * r
        up = _dot(xg, _bf16(win_ref[:, hidden + lo:hidden + hi])) * r
        acts.append((gate * jax.nn.sigmoid(gate) * up).astype(BF16))
    y = x
    for c in range(chunks):
        y = y + _dot(acts[c], _bf16(wo_ref[bounds[c]:bounds[c + 1], :]))
    o_ref[0] = _rms(y, gfin_ref[...]) if final_norm else y


def _ffn(x, gf, win, wo, gfin, final_norm):
    b, s, d = x.shape
    tm = TOKEN_TILE
    row = lambda bi, i: (bi, i, 0)
    chunks = 2
    assert win.shape[1] == 2 * wo.shape[0] and wo.shape[0] % LANES == 0
    return pl.pallas_call(
        functools.partial(_ffn_kernel, chunks=chunks, final_norm=final_norm),
        grid=(b, s // tm),
        in_specs=[pl.BlockSpec((1, tm, d), row), _const_spec(gf.shape), _const_spec(win.shape),
                  _const_spec(wo.shape), _const_spec(gfin.shape)],
        out_specs=pl.BlockSpec((1, tm, d), row),
        out_shape=jax.ShapeDtypeStruct((b, s, d), F32),
        compiler_params=pltpu.CompilerParams(
            dimension_semantics=("parallel", "parallel"), vmem_limit_bytes=VMEM_LIMIT),
        name="ffn",
    )(x, gf, win, wo, gfin)


def kernel(x, mem, norm_mix, w_in, conv_w, w_branch_a, w_branch_b, w_mix_out, norm_mem_q,
           norm_mem_kv, w_mem_q, w_mem_kv, w_mem_o, norm_ffn, w_ffn_in, w_ffn_out, norm_final):
    depth = w_in.shape[0]
    assert x.shape[1] % TOKEN_TILE == 0 and TOKEN_TILE % SB_TILE == 0
    assert LANES == 2 * SB_HEAD_DIM

    idx = jnp.arange(SB_TILE)
    tri = (idx[None, :] >= idx[:, None]).astype(BF16)

    for l in range(depth):
        qt, k, vt, sa, mb = _mix_in(x, norm_mix[l][None, :], w_in[l], conv_w[l], w_branch_b[l])
        oa = _sb_attn(qt, k, vt, tri)
        kmt, vm = _mem_kv(mem, norm_mem_kv[l][None, :], w_mem_kv[l])
        x = _mix_out(x, oa, sa, mb, w_branch_a[l], w_mix_out[l], norm_mem_q[l][None, :],
                     w_mem_q[l], kmt, vm, w_mem_o[l])
        x = _ffn(x, norm_ffn[l][None, :], w_ffn_in[l], w_ffn_out[l],
                 norm_final[None, :], final_norm=(l == depth - 1))
    return x
```

```python
import functools
import math

import jax
import jax.numpy as jnp
from jax import lax
from jax.experimental import pallas as pl
from jax.experimental.pallas import tpu as pltpu

EPS = 1e-6
SB_HEADS = 8
SB_HEAD_DIM = 64
CONV_K = 3
MEM_HEADS = 4

LANES = 128
SUBLANES = 8
MXU_DIM = 256
TOKEN_TILE = 512
SB_TILE = 256
MIB = 1024 * 1024
VMEM_LIMIT = 56 * MIB

LOG2E = math.log2(math.e)
MASKED = -1e30
DEAD_LOG2 = -160.0

BF16 = jnp.bfloat16
F32 = jnp.float32


def _dot(a, b):
    return jnp.dot(a, b, preferred_element_type=F32)


def _dot_tn(a, b):
    return lax.dot_general(a, b, (((0,), (1,)), ((), ())), preferred_element_type=F32)


def _bf16(w):
    return w.astype(BF16)


def _rms_scale(x):
    return lax.rsqrt(jnp.mean(x * x, axis=-1, keepdims=True) + EPS)


def _rms(x, g):
    return x * _rms_scale(x) * g


def _const_spec(shape):
    zeros = (0,) * len(shape)
    return pl.BlockSpec(shape, lambda *_: zeros, pipeline_mode=pl.Buffered(1))


def _mix_in_kernel(x_ref, g_ref, win_ref, cw_ref, wbb_ref,
                   qt_ref, k_ref, vt_ref, sa_ref, mb_ref, cu_buf):
    tm = x_ref.shape[1]
    d = x_ref.shape[2]
    cw = wbb_ref.shape[0]
    sbw = k_ref.shape[2]
    halo = SUBLANES
    o_conv = 3 * sbw
    o_ga = o_conv + 3 * cw
    o_gb = o_ga + d

    @pl.when(pl.program_id(1) == 0)
    def _():
        cu_buf[0:halo, :] = jnp.zeros((halo, cw), F32)

    h = _rms(x_ref[0], g_ref[...]).astype(BF16)

    ugc = _dot(h, _bf16(win_ref[:, o_conv:o_ga]))
    cu = ugc[:, 2 * cw:3 * cw] * ugc[:, 0:cw]
    gate_b = ugc[:, cw:2 * cw]
    cu_buf[halo:halo + tm, :] = cu
    conv = cw_ref[2:3, :] * cu
    for i in range(CONV_K - 1):
        shift = CONV_K - 1 - i
        conv = conv + cw_ref[i:i + 1, :] * cu_buf[halo - shift:halo - shift + tm, :]
    cu_buf[0:halo, :] = cu[tm - halo:tm, :]
    yb = (gate_b * conv).astype(BF16)

    sa_ref[0] = jax.nn.sigmoid(_dot(h, _bf16(win_ref[:, o_ga:o_gb]))).astype(BF16)
    sig_b = jax.nn.sigmoid(_dot(h, _bf16(win_ref[:, o_gb:o_gb + d])))
    mb_ref[0] = (sig_b * _dot(yb, _bf16(wbb_ref[...]))).astype(BF16)

    qt = _dot_tn(_bf16(win_ref[:, 0:sbw]), h) * (LOG2E * SB_HEAD_DIM ** -0.5)
    qt = qt.astype(BF16)
    vt = _dot_tn(_bf16(win_ref[:, 2 * sbw:3 * sbw]), h).astype(BF16)
    for p in range(qt_ref.shape[1]):
        for j in range(qt_ref.shape[2]):
            rows, cols = slice(p * LANES, (p + 1) * LANES), slice(j * SB_TILE, (j + 1) * SB_TILE)
            qt_ref[0, p, j] = qt[rows, cols]
            vt_ref[0, p, j] = vt[rows, cols]
    k_ref[0] = _dot(h, _bf16(win_ref[:, sbw:2 * sbw])).astype(BF16)


def _mix_in(x, g, win, cw, wbb):
    b, s, d = x.shape
    tm = TOKEN_TILE
    sbw = SB_HEADS * SB_HEAD_DIM
    cwid = wbb.shape[0]
    assert win.shape[1] == 3 * sbw + 3 * cwid + 2 * d
    pairs = sbw // LANES
    tiled = jax.ShapeDtypeStruct((b, pairs, s // SB_TILE, LANES, SB_TILE), BF16)
    out_shape = (
        tiled,
        jax.ShapeDtypeStruct((b, s, sbw), BF16),
        tiled,
        jax.ShapeDtypeStruct((b, s, d), BF16),
        jax.ShapeDtypeStruct((b, s, d), BF16),
    )
    row = lambda bi, i: (bi, i, 0)
    tiled_spec = pl.BlockSpec((1, pairs, tm // SB_TILE, LANES, SB_TILE),
                              lambda bi, i: (bi, 0, i, 0, 0))
    return pl.pallas_call(
        _mix_in_kernel,
        grid=(b, s // tm),
        in_specs=[
            pl.BlockSpec((1, tm, d), row),
            _const_spec(g.shape), _const_spec(win.shape), _const_spec(cw.shape),
            _const_spec(wbb.shape),
        ],
        out_specs=(
            tiled_spec,
            pl.BlockSpec((1, tm, sbw), row),
            tiled_spec,
            pl.BlockSpec((1, tm, d), row),
            pl.BlockSpec((1, tm, d), row),
        ),
        out_shape=out_shape,
        scratch_shapes=[pltpu.VMEM((tm + SUBLANES, cwid), F32)],
        compiler_params=pltpu.CompilerParams(
            dimension_semantics=("arbitrary", "arbitrary"),
            vmem_limit_bytes=VMEM_LIMIT),
        name="mix_in",
    )(x, g, win, cw, wbb)


SB_STAGES = 5
SB_STAGE_GAP = 1
SB_SLOTS = SB_STAGE_GAP * (SB_STAGES - 1)


def _sb_attn_kernel(qt_ref, k_ref, vt_ref, tri_ref, o_ref,
                    qm_ref, c_ref, acc_ref, z_buf, p_buf, cum_buf, w_buf):
    nq, tq = qt_ref.shape[2], qt_ref.shape[4]
    tk = SB_TILE
    hd = SB_HEAD_DIM
    dummy = nq

    def prep(i, carry):
        qt = qt_ref[0, 0, i]
        zero = jnp.zeros((hd, tq), BF16)
        qm_ref[0, i, 0:hd, :] = qt[0:hd]
        qm_ref[0, i, hd:2 * hd, :] = zero
        qm_ref[1, i, 0:hd, :] = zero
        qm_ref[1, i, hd:2 * hd, :] = qt[hd:2 * hd]
        return carry

    lax.fori_loop(0, nq, prep, 0)
    qm_ref[:, dummy] = jnp.zeros((2, 2 * hd, tq), BF16)
    @pl.when(jnp.logical_and(pl.program_id(0) == 0, pl.program_id(1) == 0))
    def _():
        z_buf[...] = jnp.zeros(z_buf.shape, F32)
        p_buf[...] = jnp.zeros(p_buf.shape, BF16)
        cum_buf[...] = jnp.zeros(cum_buf.shape, F32)
        w_buf[...] = jnp.zeros(w_buf.shape, BF16)

    key = lax.broadcasted_iota(jnp.int32, (tk, tq), 0)
    qry = lax.broadcasted_iota(jnp.int32, (tk, tq), 1)
    causal = key < qry

    def stage_ma(i, j, h, slot):
        kt = k_ref[0, pl.ds(pl.multiple_of(j * tk, tk), tk), :]
        z_buf[slot] = _dot(kt, qm_ref[h, i])

    def softplus2(z):
        p = jnp.maximum(z, 0.0) + jnp.log(1.0 + jnp.exp2(-jnp.abs(z))) * LOG2E
        return p.astype(BF16)

    def stage_ea(slot, masked):
        z = z_buf[slot]
        if masked:
            z = jnp.where(causal, z, MASKED)
            z_buf[slot] = z
        p_buf[slot] = softplus2(z)

    def stage_mb(slot):
        cum_buf[slot] = _dot(tri_ref[...], p_buf[slot])

    def stage_eb(i, h, slot, masked):
        cum = cum_buf[slot]
        if masked:
            w_buf[slot] = jnp.exp2(z_buf[slot] - cum).astype(BF16)
            c_ref[h, i] = jnp.broadcast_to(-cum[0:1, :], (SUBLANES, tq))
        else:
            c = c_ref[h, i, 0:1, :]
            w_buf[slot] = jnp.exp2(z_buf[slot] - cum + c).astype(BF16)
            c_ref[h, i] = jnp.broadcast_to(c - cum[0:1, :], (SUBLANES, tq))

    def stage_mc(i, j, h, slot, masked):
        vt = vt_ref[0, 0, j, h * hd:(h + 1) * hd, :]
        out = _dot(vt, w_buf[slot])
        if masked:
            acc_ref[h, i] = out
        else:
            acc_ref[h, i] += out

    def block(item, diagonal):
        def body(u, carry):
            for r in range(SB_SLOTS):
                s = SB_SLOTS * u + r

                def in_stage(k):
                    back = k * SB_STAGE_GAP
                    return item(s - back), (r - back) % 2, (r - back) % SB_SLOTS

                (i4, j4), h4, slot4 = in_stage(4)
                stage_mc(i4, j4, h4, slot4, diagonal(4, r))
                (i3, _), h3, slot3 = in_stage(3)
                stage_eb(i3, h3, slot3, diagonal(3, r))
                stage_mb(in_stage(2)[2])
                stage_ea(in_stage(1)[2], diagonal(1, r))
                (i0, j0), h0, slot0 = in_stage(0)
                stage_ma(i0, j0, h0, slot0)
            return carry
        return body

    drain = SB_STAGE_GAP * (SB_STAGES - 1)

    def blocks_for(steps):
        return (steps + SB_SLOTS - 1) // SB_SLOTS

    def run_head():
        n_diag, n_next = 2 * nq, 2 * (nq - 1)

        def item(m):
            t0, t1 = m // 2, (m - n_diag) // 2
            in_diag = jnp.logical_and(m >= 0, m < n_diag)
            in_next = jnp.logical_and(m >= n_diag, m < n_diag + n_next)
            i = jnp.where(in_diag, t0, jnp.where(in_next, t1 + 1, dummy))
            j = jnp.where(in_diag, t0, jnp.where(in_next, t1, 0))
            return i, j

        turn = n_diag // SB_SLOTS
        lax.fori_loop(0, turn, block(item, lambda k, r: True), 0)
        block(item, lambda k, r: r < k * SB_STAGE_GAP)(turn, 0)
        lax.fori_loop(turn + 1, blocks_for(n_diag + n_next + drain),
                      block(item, lambda k, r: False), 0)

    def run_distance(d):
        n_items = 2 * (nq - d)

        def item(m):
            valid = jnp.logical_and(m >= 0, m < n_items)
            t = m // 2
            return jnp.where(valid, d + t, dummy), jnp.where(valid, t, 0)

        lax.fori_loop(0, blocks_for(n_items + drain), block(item, lambda k, r: False), 0)

    def alive_after(d):
        cc = jnp.maximum(c_ref[0, 0:nq], c_ref[1, 0:nq])
        blk = lax.broadcasted_iota(jnp.int32, cc.shape, 0)
        top = jnp.max(jnp.where(blk > d, cc, MASKED), axis=0)
        return (jnp.max(top) > DEAD_LOG2).astype(jnp.int32)

    run_head()

    def more(state):
        d, alive = state
        return jnp.logical_and(d < nq, alive > 0)

    def step(state):
        d, _ = state
        run_distance(d)
        return d + 1, alive_after(d)

    lax.while_loop(more, step, (jnp.int32(2), alive_after(1)))

    for h in range(2):
        o_ref[0, 0, :, h * hd:(h + 1) * hd, :] = acc_ref[h, 0:nq].astype(BF16)


def _sb_attn(qt, k, vt, tri):
    b, pairs, nq, _, tq = qt.shape
    s, sbw = k.shape[1], k.shape[2]
    tk = SB_TILE
    assert nq >= 2 and (2 * nq) % SB_SLOTS == 0 and 2 * nq - 2 > SB_STAGE_GAP * (SB_STAGES - 1)
    return pl.pallas_call(
        _sb_attn_kernel,
        grid=(b, pairs),
        in_specs=[
            pl.BlockSpec((1, 1, nq, LANES, tq), lambda bi, p: (bi, p, 0, 0, 0)),
            pl.BlockSpec((1, s, LANES), lambda bi, p: (bi, 0, p)),
            pl.BlockSpec((1, 1, s // tk, LANES, tk), lambda bi, p: (bi, p, 0, 0, 0)),
            _const_spec(tri.shape),
        ],
        out_specs=pl.BlockSpec((1, 1, nq, LANES, tq), lambda bi, p: (bi, p, 0, 0, 0)),
        out_shape=jax.ShapeDtypeStruct(qt.shape, BF16),
        scratch_shapes=[
            pltpu.VMEM((2, nq + 1, LANES, tq), BF16),
            pltpu.VMEM((2, nq + 1, SUBLANES, tq), F32),
            pltpu.VMEM((2, nq + 1, SB_HEAD_DIM, tq), F32),
            pltpu.VMEM((SB_SLOTS, tk, tq), F32),
            pltpu.VMEM((SB_SLOTS, tk, tq), BF16),
            pltpu.VMEM((SB_SLOTS, tk, tq), F32),
            pltpu.VMEM((SB_SLOTS, tk, tq), BF16),
        ],
        compiler_params=pltpu.CompilerParams(
            dimension_semantics=("arbitrary", "arbitrary"),
            vmem_limit_bytes=VMEM_LIMIT),
        name="sb_attn",
    )(qt, k, vt, tri)


def _mem_kv_kernel(mem_ref, g_ref, wkv_ref, kt_ref, v_ref):
    nb, m, d = mem_ref.shape
    hm = _rms(mem_ref[...].reshape(nb * m, d), g_ref[...]).astype(BF16)
    kt = _dot_tn(_bf16(wkv_ref[:, 0:d]), hm).astype(BF16)
    v = _dot(hm, _bf16(wkv_ref[:, d:2 * d])).astype(BF16)
    dh = kt_ref.shape[2]
    for bi in range(nb):
        for hd in range(kt_ref.shape[1]):
            kt_ref[bi, hd] = kt[hd * dh:(hd + 1) * dh, bi * m:(bi + 1) * m]
        v_ref[bi] = v[bi * m:(bi + 1) * m]


def _mem_kv(mem, g, wkv):
    b, m, d = mem.shape
    dh = d // MEM_HEADS
    kt_shape, v_shape = (b, MEM_HEADS, dh, m), (b, m, d)
    return pl.pallas_call(
        _mem_kv_kernel,
        grid=(1,),
        in_specs=[_const_spec(mem.shape), _const_spec(g.shape), _const_spec(wkv.shape)],
        out_specs=(pl.BlockSpec(kt_shape, lambda i: (0, 0, 0, 0)),
                   pl.BlockSpec(v_shape, lambda i: (0, 0, 0))),
        out_shape=(jax.ShapeDtypeStruct(kt_shape, BF16), jax.ShapeDtypeStruct(v_shape, BF16)),
        compiler_params=pltpu.CompilerParams(
            dimension_semantics=("arbitrary",), vmem_limit_bytes=VMEM_LIMIT),
        name="mem_kv",
    )(mem, g, wkv)


def _mix_out_kernel(x_ref, oa_ref, sa_ref, mb_ref, wba_ref, wmo_ref, gq_ref, wmq_ref,
                    kmt_ref, vm_ref, wmemo_ref, o_ref):
    oat = jnp.concatenate(
        [jnp.concatenate([oa_ref[0, p, j] for j in range(oa_ref.shape[2])], axis=1)
         for p in range(oa_ref.shape[1])], axis=0)
    branch_a = lax.dot_general(oat, _bf16(wba_ref[...]), (((0,), (0,)), ((), ())),
                               preferred_element_type=F32)
    merged =(sa_ref[0].astype(F32) * branch_a + mb_ref[0].astype(F32)).astype(BF16)
    x1 = x_ref[0] + _dot(merged, _bf16(wmo_ref[...]))

    dh = kmt_ref.shape[2]
    xg = (x1 * gq_ref[...]).astype(BF16)
    qm = (_dot(xg, _bf16(wmq_ref[...])) * (_rms_scale(x1) * (dh ** -0.5))).astype(BF16)
    cols = [slice(hd * dh, (hd + 1) * dh) for hd in range(MEM_HEADS)]
    scores = [_dot(qm[:, c], kmt_ref[0, hd]) for hd, c in enumerate(cols)]
    heads = []
    for sc, c in zip(scores, cols):
        e = jnp.exp(sc - jnp.max(sc, axis=-1, keepdims=True))
        denom = jnp.sum(e, axis=-1, keepdims=True)
        o = _dot(e.astype(BF16), vm_ref[0, :, c]) / denom
        heads.append(o.astype(BF16))
    o_ref[0] = x1 + _dot(jnp.concatenate(heads, axis=1), _bf16(wmemo_ref[...]))


def _mix_out(x, oa, sa, mb, wba, wmo, gq, wmq, kmt, vm, wmemo):
    b, s, d = x.shape
    tm = TOKEN_TILE
    row = lambda bi, i: (bi, i, 0)
    return pl.pallas_call(
        _mix_out_kernel,
        grid=(b, s // tm),
        in_specs=[
            pl.BlockSpec((1, tm, d), row),
            pl.BlockSpec((1, oa.shape[1], tm // oa.shape[4], LANES, oa.shape[4]),
                         lambda bi, i: (bi, 0, i, 0, 0)),
            pl.BlockSpec((1, tm, d), row),
            pl.BlockSpec((1, tm, d), row),
            _const_spec(wba.shape), _const_spec(wmo.shape), _const_spec(gq.shape),
            _const_spec(wmq.shape),
            pl.BlockSpec((1,) + kmt.shape[1:], lambda bi, i: (bi, 0, 0, 0)),
            pl.BlockSpec((1,) + vm.shape[1:], lambda bi, i: (bi, 0, 0)),
            _const_spec(wmemo.shape),
        ],
        out_specs=pl.BlockSpec((1, tm, d), row),
        out_shape=jax.ShapeDtypeStruct((b, s, d), F32),
        compiler_params=pltpu.CompilerParams(
            dimension_semantics=("parallel", "parallel"), vmem_limit_bytes=VMEM_LIMIT),
        name="mix_out",
    )(x, oa, sa, mb, wba, wmo, gq, wmq, kmt, vm, wmemo)


def _ffn_kernel(x_ref, gf_ref, win_ref, wo_ref, gfin_ref, o_ref, *, chunks, final_norm):
    hidden = wo_ref.shape[0]
    tiles = pl.cdiv(hidden, MXU_DIM)
    bounds = [min(hidden, (c * tiles // chunks) * MXU_DIM) for c in range(chunks + 1)]
    x = x_ref[0]
    xg = (x * gf_ref[...]).astype(BF16)
    r = _rms_scale(x)
    acts = []
    for c in range(chunks):
        lo, hi = bounds[c], bounds[c + 1]
        gate = _dot(xg, _bf16(win_ref[:, lo:hi])) * r
        up = _dot(xg, _bf16(win_ref[:, hidden + lo:hidden + hi])) * r
        acts.append((gate * jax.nn.sigmoid(gate) * up).astype(BF16))
    y = x
    for c in range(chunks):
        y = y + _dot(acts[c], _bf16(wo_ref[bounds[c]:bounds[c + 1], :]))
    o_ref[0] = _rms(y, gfin_ref[...]) if final_norm else y


def _ffn(x, gf, win, wo, gfin, final_norm):
    b, s, d = x.shape
    tm = TOKEN_TILE
    row = lambda bi, i: (bi, i, 0)
    chunks = 2
    assert win.shape[1] == 2 * wo.shape[0] and wo.shape[0] % LANES == 0
    return pl.pallas_call(
        functools.partial(_ffn_kernel, chunks=chunks, final_norm=final_norm),
        grid=(b, s // tm),
        in_specs=[pl.BlockSpec((1, tm, d), row), _const_spec(gf.shape), _const_spec(win.shape),
                  _const_spec(wo.shape), _const_spec(gfin.shape)],
        out_specs=pl.BlockSpec((1, tm, d), row),
        out_shape=jax.ShapeDtypeStruct((b, s, d), F32),
        compiler_params=pltpu.CompilerParams(
            dimension_semantics=("parallel", "parallel"), vmem_limit_bytes=VMEM_LIMIT),
        name="ffn",
    )(x, gf, win, wo, gfin)


def kernel(x, mem, norm_mix, w_in, conv_w, w_branch_a, w_branch_b, w_mix_out, norm_mem_q,
           norm_mem_kv, w_mem_q, w_mem_kv, w_mem_o, norm_ffn, w_ffn_in, w_ffn_out, norm_final):
    depth = w_in.shape[0]
    assert x.shape[1] % TOKEN_TILE == 0 and TOKEN_TILE % SB_TILE == 0
    assert LANES == 2 * SB_HEAD_DIM

    idx = jnp.arange(SB_TILE)
    tri = (idx[None, :] >= idx[:, None]).astype(BF16)

    for l in range(depth):
        qt, k, vt, sa, mb = _mix_in(x, norm_mix[l][None, :], w_in[l], conv_w[l], w_branch_b[l])
        oa = _sb_attn(qt, k, vt, tri)
        kmt, vm = _mem_kv(mem, norm_mem_kv[l][None, :], w_mem_kv[l])
        x = _mix_out(x, oa, sa, mb, w_branch_a[l], w_mix_out[l], norm_mem_q[l][None, :],
                     w_mem_q[l], kmt, vm, w_mem_o[l])
        x = _ffn(x, norm_ffn[l][None, :], w_ffn_in[l], w_ffn_out[l],
                 norm_final[None, :], final_norm=(l == depth - 1))
    return x
```

```python
import functools
import math

import jax
import jax.numpy as jnp
from jax import lax
from jax.experimental import pallas as pl
from jax.experimental.pallas import tpu as pltpu

EPS = 1e-6
SB_HEADS = 8
SB_HEAD_DIM = 64
CONV_K = 3
MEM_HEADS = 4

LANES = 128
SUBLANES = 8
MXU_DIM = 256
FFN_TOKEN_TILE = 512
MIX_TOKEN_TILE = 1024
SB_TILE = 256
MIB = 1024 * 1024
VMEM_LIMIT = 56 * MIB

LOG2E = math.log2(math.e)
MASKED = -1e30
DEAD_LOG2 = -160.0

BF16 = jnp.bfloat16
F32 = jnp.float32


def _dot(a, b):
    return jnp.dot(a, b, preferred_element_type=F32)


def _dot_tn(a, b):
    return lax.dot_general(a, b, (((0,), (1,)), ((), ())), preferred_element_type=F32)


def _bf16(w):
    return w.astype(BF16)


def _rms_scale(x):
    return lax.rsqrt(jnp.mean(x * x, axis=-1, keepdims=True) + EPS)


def _rms(x, g):
    return x * _rms_scale(x) * g


def _const_spec(shape):
    zeros = (0,) * len(shape)
    return pl.BlockSpec(shape, lambda *_: zeros, pipeline_mode=pl.Buffered(1))


def _mix_in_kernel(x_ref, g_ref, win_ref, cw_ref, wbb_ref,
                   qt_ref, k_ref, vt_ref, sa_ref, mb_ref, cu_buf):
    tm = x_ref.shape[1]
    d = x_ref.shape[2]
    cw = wbb_ref.shape[0]
    sbw = k_ref.shape[2]
    halo = SUBLANES
    o_conv = 3 * sbw
    o_ga = o_conv + 3 * cw
    o_gb = o_ga + d

    @pl.when(pl.program_id(1) == 0)
    def _():
        cu_buf[0:halo, :] = jnp.zeros((halo, cw), F32)

    h = _rms(x_ref[0], g_ref[...]).astype(BF16)

    ugc = _dot(h, _bf16(win_ref[:, o_conv:o_ga]))
    cu = ugc[:, 2 * cw:3 * cw] * ugc[:, 0:cw]
    gate_b = ugc[:, cw:2 * cw]
    cu_buf[halo:halo + tm, :] = cu
    conv = cw_ref[2:3, :] * cu
    for i in range(CONV_K - 1):
        shift = CONV_K - 1 - i
        conv = conv + cw_ref[i:i + 1, :] * cu_buf[halo - shift:halo - shift + tm, :]
    cu_buf[0:halo, :] = cu[tm - halo:tm, :]
    yb = (gate_b * conv).astype(BF16)

    sa_ref[0] = jax.nn.sigmoid(_dot(h, _bf16(win_ref[:, o_ga:o_gb]))).astype(BF16)
    sig_b = jax.nn.sigmoid(_dot(h, _bf16(win_ref[:, o_gb:o_gb + d])))
    mb_ref[0] = (sig_b * _dot(yb, _bf16(wbb_ref[...]))).astype(BF16)

    qt = _dot_tn(_bf16(win_ref[:, 0:sbw]), h) * (LOG2E * SB_HEAD_DIM ** -0.5)
    qt = qt.astype(BF16)
    vt = _dot_tn(_bf16(win_ref[:, 2 * sbw:3 * sbw]), h).astype(BF16)
    for p in range(qt_ref.shape[1]):
        for j in range(qt_ref.shape[2]):
            rows, cols = slice(p * LANES, (p + 1) * LANES), slice(j * SB_TILE, (j + 1) * SB_TILE)
            qt_ref[0, p, j] = qt[rows, cols]
            vt_ref[0, p, j] = vt[rows, cols]
    k_ref[0] = _dot(h, _bf16(win_ref[:, sbw:2 * sbw])).astype(BF16)


def _mix_in(x, g, win, cw, wbb):
    b, s, d = x.shape
    tm = MIX_TOKEN_TILE
    sbw = SB_HEADS * SB_HEAD_DIM
    cwid = wbb.shape[0]
    assert win.shape[1] == 3 * sbw + 3 * cwid + 2 * d
    pairs = sbw // LANES
    tiled = jax.ShapeDtypeStruct((b, pairs, s // SB_TILE, LANES, SB_TILE), BF16)
    out_shape = (
        tiled,
        jax.ShapeDtypeStruct((b, s, sbw), BF16),
        tiled,
        jax.ShapeDtypeStruct((b, s, d), BF16),
        jax.ShapeDtypeStruct((b, s, d), BF16),
    )
    row = lambda bi, i: (bi, i, 0)
    tiled_spec = pl.BlockSpec((1, pairs, tm // SB_TILE, LANES, SB_TILE),
                              lambda bi, i: (bi, 0, i, 0, 0))
    return pl.pallas_call(
        _mix_in_kernel,
        grid=(b, s // tm),
        in_specs=[
            pl.BlockSpec((1, tm, d), row),
            _const_spec(g.shape), _const_spec(win.shape), _const_spec(cw.shape),
            _const_spec(wbb.shape),
        ],
        out_specs=(
            tiled_spec,
            pl.BlockSpec((1, tm, sbw), row),
            tiled_spec,
            pl.BlockSpec((1, tm, d), row),
            pl.BlockSpec((1, tm, d), row),
        ),
        out_shape=out_shape,
        scratch_shapes=[pltpu.VMEM((tm + SUBLANES, cwid), F32)],
        compiler_params=pltpu.CompilerParams(
            dimension_semantics=("arbitrary", "arbitrary"),
            vmem_limit_bytes=VMEM_LIMIT),
        name="mix_in",
    )(x, g, win, cw, wbb)


SB_STAGES = 5
SB_STAGE_GAP = 1
SB_SLOTS = SB_STAGE_GAP * (SB_STAGES - 1)


def _sb_attn_kernel(qt_ref, k_ref, vt_ref, tri_ref, o_ref,
                    qm_ref, c_ref, acc_ref, z_buf, p_buf, cum_buf, w_buf):
    nq, tq = qt_ref.shape[2], qt_ref.shape[4]
    tk = SB_TILE
    hd = SB_HEAD_DIM
    dummy = nq

    def prep(i, carry):
        qt = qt_ref[0, 0, i]
        zero = jnp.zeros((hd, tq), BF16)
        qm_ref[0, i, 0:hd, :] = qt[0:hd]
        qm_ref[0, i, hd:2 * hd, :] = zero
        qm_ref[1, i, 0:hd, :] = zero
        qm_ref[1, i, hd:2 * hd, :] = qt[hd:2 * hd]
        return carry

    lax.fori_loop(0, nq, prep, 0)
    qm_ref[:, dummy] = jnp.zeros((2, 2 * hd, tq), BF16)
    @pl.when(jnp.logical_and(pl.program_id(0) == 0, pl.program_id(1) == 0))
    def _():
        z_buf[...] = jnp.zeros(z_buf.shape, F32)
        p_buf[...] = jnp.zeros(p_buf.shape, BF16)
        cum_buf[...] = jnp.zeros(cum_buf.shape, F32)
        w_buf[...] = jnp.zeros(w_buf.shape, BF16)

    key = lax.broadcasted_iota(jnp.int32, (tk, tq), 0)
    qry = lax.broadcasted_iota(jnp.int32, (tk, tq), 1)
    causal = key < qry

    def stage_ma(i, j, h, slot):
        kt = k_ref[0, pl.ds(pl.multiple_of(j * tk, tk), tk), :]
        z_buf[slot] = _dot(kt, qm_ref[h, i])

    def softplus2(z):
        p = jnp.maximum(z, 0.0) + jnp.log(1.0 + jnp.exp2(-jnp.abs(z))) * LOG2E
        return p.astype(BF16)

    def stage_ea(slot, masked):
        z = z_buf[slot]
        if masked:
            z = jnp.where(causal, z, MASKED)
            z_buf[slot] = z
        p_buf[slot] = softplus2(z)

    def stage_mb(slot):
        cum_buf[slot] = _dot(tri_ref[...], p_buf[slot])

    def stage_eb(i, h, slot, masked):
        cum = cum_buf[slot]
        if masked:
            w_buf[slot] = jnp.exp2(z_buf[slot] - cum).astype(BF16)
            c_ref[h, i] = jnp.broadcast_to(-cum[0:1, :], (SUBLANES, tq))
        else:
            c = c_ref[h, i, 0:1, :]
            w_buf[slot] = jnp.exp2(z_buf[slot] - cum + c).astype(BF16)
            c_ref[h, i] = jnp.broadcast_to(c - cum[0:1, :], (SUBLANES, tq))

    def stage_mc(i, j, h, slot, masked):
        vt = vt_ref[0, 0, j, h * hd:(h + 1) * hd, :]
        out = _dot(vt, w_buf[slot])
        if masked:
            acc_ref[h, i] = out
        else:
            acc_ref[h, i] += out

    def block(item, diagonal):
        def body(u, carry):
            for r in range(SB_SLOTS):
                s = SB_SLOTS * u + r

                def in_stage(k):
                    back = k * SB_STAGE_GAP
                    return item(s - back), (r - back) % 2, (r - back) % SB_SLOTS

                (i4, j4), h4, slot4 = in_stage(4)
                stage_mc(i4, j4, h4, slot4, diagonal(4, r))
                (i3, _), h3, slot3 = in_stage(3)
                stage_eb(i3, h3, slot3, diagonal(3, r))
                stage_mb(in_stage(2)[2])
                stage_ea(in_stage(1)[2], diagonal(1, r))
                (i0, j0), h0, slot0 = in_stage(0)
                stage_ma(i0, j0, h0, slot0)
            return carry
        return body

    drain = SB_STAGE_GAP * (SB_STAGES - 1)

    def blocks_for(steps):
        return (steps + SB_SLOTS - 1) // SB_SLOTS

    def run_head():
        n_diag, n_next = 2 * nq, 2 * (nq - 1)

        def item(m):
            t0, t1 = m // 2, (m - n_diag) // 2
            in_diag = jnp.logical_and(m >= 0, m < n_diag)
            in_next = jnp.logical_and(m >= n_diag, m < n_diag + n_next)
            i = jnp.where(in_diag, t0, jnp.where(in_next, t1 + 1, dummy))
            j = jnp.where(in_diag, t0, jnp.where(in_next, t1, 0))
            return i, j

        turn = n_diag // SB_SLOTS
        lax.fori_loop(0, turn, block(item, lambda k, r: True), 0)
        block(item, lambda k, r: r < k * SB_STAGE_GAP)(turn, 0)
        lax.fori_loop(turn + 1, blocks_for(n_diag + n_next + drain),
                      block(item, lambda k, r: False), 0)

    def run_distance(d):
        n_items = 2 * (nq - d)

        def item(m):
            valid = jnp.logical_and(m >= 0, m < n_items)
            t = m // 2
            return jnp.where(valid, d + t, dummy), jnp.where(valid, t, 0)

        lax.fori_loop(0, blocks_for(n_items + drain), block(item, lambda k, r: False), 0)

    def alive_after(d):
        cc = jnp.maximum(c_ref[0, 0:nq], c_ref[1, 0:nq])
        blk = lax.broadcasted_iota(jnp.int32, cc.shape, 0)
        top = jnp.max(jnp.where(blk > d, cc, MASKED), axis=0)
        return (jnp.max(top) > DEAD_LOG2).astype(jnp.int32)

    run_head()

    def more(state):
        d, alive = state
        return jnp.logical_and(d < nq, alive > 0)

    def step(state):
        d, _ = state
        run_distance(d)
        return d + 1, alive_after(d)

    lax.while_loop(more, step, (jnp.int32(2), alive_after(1)))

    for h in range(2):
        o_ref[0, 0, :, h * hd:(h + 1) * hd, :] = acc_ref[h, 0:nq].astype(BF16)


def _sb_attn(qt, k, vt, tri):
    b, pairs, nq, _, tq = qt.shape
    s, sbw = k.shape[1], k.shape[2]
    tk = SB_TILE
    assert nq >= 2 and (2 * nq) % SB_SLOTS == 0 and 2 * nq - 2 > SB_STAGE_GAP * (SB_STAGES - 1)
    return pl.pallas_call(
        _sb_attn_kernel,
        grid=(b, pairs),
        in_specs=[
            pl.BlockSpec((1, 1, nq, LANES, tq), lambda bi, p: (bi, p, 0, 0, 0)),
            pl.BlockSpec((1, s, LANES), lambda bi, p: (bi, 0, p)),
            pl.BlockSpec((1, 1, s // tk, LANES, tk), lambda bi, p: (bi, p, 0, 0, 0)),
            _const_spec(tri.shape),
        ],
        out_specs=pl.BlockSpec((1, 1, nq, LANES, tq), lambda bi, p: (bi, p, 0, 0, 0)),
        out_shape=jax.ShapeDtypeStruct(qt.shape, BF16),
        scratch_shapes=[
            pltpu.VMEM((2, nq + 1, LANES, tq), BF16),
            pltpu.VMEM((2, nq + 1, SUBLANES, tq), F32),
            pltpu.VMEM((2, nq + 1, SB_HEAD_DIM, tq), F32),
            pltpu.VMEM((SB_SLOTS, tk, tq), F32),
            pltpu.VMEM((SB_SLOTS, tk, tq), BF16),
            pltpu.VMEM((SB_SLOTS, tk, tq), F32),
            pltpu.VMEM((SB_SLOTS, tk, tq), BF16),
        ],
        compiler_params=pltpu.CompilerParams(
            dimension_semantics=("arbitrary", "arbitrary"),
            vmem_limit_bytes=VMEM_LIMIT),
        name="sb_attn",
    )(qt, k, vt, tri)


def _mem_kv_kernel(mem_ref, g_ref, wkv_ref, kt_ref, v_ref):
    nb, m, d = mem_ref.shape
    hm = _rms(mem_ref[...].reshape(nb * m, d), g_ref[...]).astype(BF16)
    kt = _dot_tn(_bf16(wkv_ref[:, 0:d]), hm).astype(BF16)
    v = _dot(hm, _bf16(wkv_ref[:, d:2 * d])).astype(BF16)
    dh = kt_ref.shape[2]
    for bi in range(nb):
        for hd in range(kt_ref.shape[1]):
            kt_ref[bi, hd] = kt[hd * dh:(hd + 1) * dh, bi * m:(bi + 1) * m]
        v_ref[bi] = v[bi * m:(bi + 1) * m]


def _mem_kv(mem, g, wkv):
    b, m, d = mem.shape
    dh = d // MEM_HEADS
    kt_shape, v_shape = (b, MEM_HEADS, dh, m), (b, m, d)
    return pl.pallas_call(
        _mem_kv_kernel,
        grid=(1,),
        in_specs=[_const_spec(mem.shape), _const_spec(g.shape), _const_spec(wkv.shape)],
        out_specs=(pl.BlockSpec(kt_shape, lambda i: (0, 0, 0, 0)),
                   pl.BlockSpec(v_shape, lambda i: (0, 0, 0))),
        out_shape=(jax.ShapeDtypeStruct(kt_shape, BF16), jax.ShapeDtypeStruct(v_shape, BF16)),
        compiler_params=pltpu.CompilerParams(
            dimension_semantics=("arbitrary",), vmem_limit_bytes=VMEM_LIMIT),
        name="mem_kv",
    )(mem, g, wkv)


def _mix_out_kernel(x_ref, oa_ref, sa_ref, mb_ref, wba_ref, wmo_ref, gq_ref, wmq_ref,
                    kmt_ref, vm_ref, wmemo_ref, o_ref):
    oat = jnp.concatenate(
        [jnp.concatenate([oa_ref[0, p, j] for j in range(oa_ref.shape[2])], axis=1)
         for p in range(oa_ref.shape[1])], axis=0)
    branch_a = lax.dot_general(oat, _bf16(wba_ref[...]), (((0,), (0,)), ((), ())),
                               preferred_element_type=F32)
    merged =(sa_ref[0].astype(F32) * branch_a + mb_ref[0].astype(F32)).astype(BF16)
    x1 = x_ref[0] + _dot(merged, _bf16(wmo_ref[...]))

    dh = kmt_ref.shape[2]
    xg = (x1 * gq_ref[...]).astype(BF16)
    qm = (_dot(xg, _bf16(wmq_ref[...])) * (_rms_scale(x1) * (dh ** -0.5))).astype(BF16)
    cols = [slice(hd * dh, (hd + 1) * dh) for hd in range(MEM_HEADS)]
    scores = [_dot(qm[:, c], kmt_ref[0, hd]) for hd, c in enumerate(cols)]
    heads = []
    for sc, c in zip(scores, cols):
        e = jnp.exp(sc - jnp.max(sc, axis=-1, keepdims=True))
        denom = jnp.sum(e, axis=-1, keepdims=True)
        o = _dot(e.astype(BF16), vm_ref[0, :, c]) / denom
        heads.append(o.astype(BF16))
    o_ref[0] = x1 + _dot(jnp.concatenate(heads, axis=1), _bf16(wmemo_ref[...]))


def _mix_out(x, oa, sa, mb, wba, wmo, gq, wmq, kmt, vm, wmemo):
    b, s, d = x.shape
    tm = MIX_TOKEN_TILE
    row = lambda bi, i: (bi, i, 0)
    return pl.pallas_call(
        _mix_out_kernel,
        grid=(b, s // tm),
        in_specs=[
            pl.BlockSpec((1, tm, d), row),
            pl.BlockSpec((1, oa.shape[1], tm // oa.shape[4], LANES, oa.shape[4]),
                         lambda bi, i: (bi, 0, i, 0, 0)),
            pl.BlockSpec((1, tm, d), row),
            pl.BlockSpec((1, tm, d), row),
            _const_spec(wba.shape), _const_spec(wmo.shape), _const_spec(gq.shape),
            _const_spec(wmq.shape),
            pl.BlockSpec((1,) + kmt.shape[1:], lambda bi, i: (bi, 0, 0, 0)),
            pl.BlockSpec((1,) + vm.shape[1:], lambda bi, i: (bi, 0, 0)),
            _const_spec(wmemo.shape),
        ],
        out_specs=pl.BlockSpec((1, tm, d), row),
        out_shape=jax.ShapeDtypeStruct((b, s, d), F32),
        compiler_params=pltpu.CompilerParams(
            dimension_semantics=("parallel", "parallel"), vmem_limit_bytes=VMEM_LIMIT),
        name="mix_out",
    )(x, oa, sa, mb, wba, wmo, gq, wmq, kmt, vm, wmemo)


def _ffn_kernel(x_ref, gf_ref, win_ref, wo_ref, gfin_ref, o_ref, *, chunks, final_norm):
    hidden = wo_ref.shape[0]
    tiles = pl.cdiv(hidden, MXU_DIM)
    bounds = [min(hidden, (c * tiles // chunks) * MXU_DIM) for c in range(chunks + 1)]
    x = x_ref[0]
    xg = (x * gf_ref[...]).astype(BF16)
    r = _rms_scale(x)
    acts = []
    for c in range(chunks):
        lo, hi = bounds[c], bounds[c + 1]
        gate = _dot(xg, _bf16(win_ref[:, lo:hi])) * r
        up = _dot(xg, _bf16(win_ref[:, hidden + lo:hidden + hi])) * r
        acts.append((gate * jax.nn.sigmoid(gate) * up).astype(BF16))
    y = x
    for c in range(chunks):
        y = y + _dot(acts[c], _bf16(wo_ref[bounds[c]:bounds[c + 1], :]))
    o_ref[0] = _rms(y, gfin_ref[...]) if final_norm else y


def _ffn(x, gf, win, wo, gfin, final_norm):
    b, s, d = x.shape
    tm = FFN_TOKEN_TILE
    row = lambda bi, i: (bi, i, 0)
    chunks = 2
    assert win.shape[1] == 2 * wo.shape[0] and wo.shape[0] % LANES == 0
    return pl.pallas_call(
        functools.partial(_ffn_kernel, chunks=chunks, final_norm=final_norm),
        grid=(b, s // tm),
        in_specs=[pl.BlockSpec((1, tm, d), row), _const_spec(gf.shape), _const_spec(win.shape),
                  _const_spec(wo.shape), _const_spec(gfin.shape)],
        out_specs=pl.BlockSpec((1, tm, d), row),
        out_shape=jax.ShapeDtypeStruct((b, s, d), F32),
        compiler_params=pltpu.CompilerParams(
            dimension_semantics=("parallel", "parallel"), vmem_limit_bytes=VMEM_LIMIT),
        name="ffn",
    )(x, gf, win, wo, gfin)


def kernel(x, mem, norm_mix, w_in, conv_w, w_branch_a, w_branch_b, w_mix_out, norm_mem_q,
           norm_mem_kv, w_mem_q, w_mem_kv, w_mem_o, norm_ffn, w_ffn_in, w_ffn_out, norm_final):
    depth = w_in.shape[0]
    assert x.shape[1] % MIX_TOKEN_TILE == 0 and x.shape[1] % FFN_TOKEN_TILE == 0
    assert MIX_TOKEN_TILE % SB_TILE == 0
    assert LANES == 2 * SB_HEAD_DIM

    idx = jnp.arange(SB_TILE)
    tri = (idx[None, :] >= idx[:, None]).astype(BF16)

    for l in range(depth):
        qt, k, vt, sa, mb = _mix_in(x, norm_mix[l][None, :], w_in[l], conv_w[l], w_branch_b[l])
        oa = _sb_attn(qt, k, vt, tri)
        kmt, vm = _mem_kv(mem, norm_mem_kv[l][None, :], w_mem_kv[l])
        x = _mix_out(x, oa, sa, mb, w_branch_a[l], w_mix_out[l], norm_mem_q[l][None, :],
                     w_mem_q[l], kmt, vm, w_mem_o[l])
        x = _ffn(x, norm_ffn[l][None, :], w_ffn_in[l], w_ffn_out[l],
                 norm_final[None, :], final_norm=(l == depth - 1))
    return x
```

```python
import functools
import math

import jax
import jax.numpy as jnp
from jax import lax
from jax.experimental import pallas as pl
from jax.experimental.pallas import tpu as pltpu

EPS = 1e-6
SB_HEADS = 8
SB_HEAD_DIM = 64
CONV_K = 3
MEM_HEADS = 4

LANES = 128
SUBLANES = 8
MXU_DIM = 256
FFN_TOKEN_TILE = 1024
MIX_TOKEN_TILE = 1024
SB_TILE = 256
MIB = 1024 * 1024
VMEM_LIMIT = 60 * MIB

LOG2E = math.log2(math.e)
MASKED = -1e30
DEAD_LOG2 = -160.0

BF16 = jnp.bfloat16
F32 = jnp.float32


def _dot(a, b):
    return jnp.dot(a, b, preferred_element_type=F32)


def _dot_tn(a, b):
    return lax.dot_general(a, b, (((0,), (1,)), ((), ())), preferred_element_type=F32)


def _bf16(w):
    return w.astype(BF16)


def _rms_scale(x):
    return lax.rsqrt(jnp.mean(x * x, axis=-1, keepdims=True) + EPS)


def _rms(x, g):
    return x * _rms_scale(x) * g


def _const_spec(shape):
    zeros = (0,) * len(shape)
    return pl.BlockSpec(shape, lambda *_: zeros, pipeline_mode=pl.Buffered(1))


def _mix_in_kernel(x_ref, g_ref, win_ref, cw_ref, wbb_ref,
                   qt_ref, k_ref, vt_ref, sa_ref, mb_ref, cu_buf):
    tm = x_ref.shape[1]
    d = x_ref.shape[2]
    cw = wbb_ref.shape[0]
    sbw = k_ref.shape[2]
    halo = SUBLANES
    o_conv = 3 * sbw
    o_ga = o_conv + 3 * cw
    o_gb = o_ga + d

    @pl.when(pl.program_id(1) == 0)
    def _():
        cu_buf[0:halo, :] = jnp.zeros((halo, cw), F32)

    h = _rms(x_ref[0], g_ref[...]).astype(BF16)

    ugc = _dot(h, _bf16(win_ref[:, o_conv:o_ga]))
    cu = ugc[:, 2 * cw:3 * cw] * ugc[:, 0:cw]
    gate_b = ugc[:, cw:2 * cw]
    cu_buf[halo:halo + tm, :] = cu
    conv = cw_ref[2:3, :] * cu
    for i in range(CONV_K - 1):
        shift = CONV_K - 1 - i
        conv = conv + cw_ref[i:i + 1, :] * cu_buf[halo - shift:halo - shift + tm, :]
    cu_buf[0:halo, :] = cu[tm - halo:tm, :]
    yb = (gate_b * conv).astype(BF16)

    sa_ref[0] = jax.nn.sigmoid(_dot(h, _bf16(win_ref[:, o_ga:o_gb]))).astype(BF16)
    sig_b = jax.nn.sigmoid(_dot(h, _bf16(win_ref[:, o_gb:o_gb + d])))
    mb_ref[0] = (sig_b * _dot(yb, _bf16(wbb_ref[...]))).astype(BF16)

    qt = _dot_tn(_bf16(win_ref[:, 0:sbw]), h) * (LOG2E * SB_HEAD_DIM ** -0.5)
    qt = qt.astype(BF16)
    vt = _dot_tn(_bf16(win_ref[:, 2 * sbw:3 * sbw]), h).astype(BF16)
    for p in range(qt_ref.shape[1]):
        for j in range(qt_ref.shape[2]):
            rows, cols = slice(p * LANES, (p + 1) * LANES), slice(j * SB_TILE, (j + 1) * SB_TILE)
            qt_ref[0, p, j] = qt[rows, cols]
            vt_ref[0, p, j] = vt[rows, cols]
    k_ref[0] = _dot(h, _bf16(win_ref[:, sbw:2 * sbw])).astype(BF16)


def _mix_in(x, g, win, cw, wbb):
    b, s, d = x.shape
    tm = MIX_TOKEN_TILE
    sbw = SB_HEADS * SB_HEAD_DIM
    cwid = wbb.shape[0]
    assert win.shape[1] == 3 * sbw + 3 * cwid + 2 * d
    pairs = sbw // LANES
    tiled = jax.ShapeDtypeStruct((b, pairs, s // SB_TILE, LANES, SB_TILE), BF16)
    out_shape = (
        tiled,
        jax.ShapeDtypeStruct((b, s, sbw), BF16),
        tiled,
        jax.ShapeDtypeStruct((b, s, d), BF16),
        jax.ShapeDtypeStruct((b, s, d), BF16),
    )
    row = lambda bi, i: (bi, i, 0)
    tiled_spec = pl.BlockSpec((1, pairs, tm // SB_TILE, LANES, SB_TILE),
                              lambda bi, i: (bi, 0, i, 0, 0))
    return pl.pallas_call(
        _mix_in_kernel,
        grid=(b, s // tm),
        in_specs=[
            pl.BlockSpec((1, tm, d), row),
            _const_spec(g.shape), _const_spec(win.shape), _const_spec(cw.shape),
            _const_spec(wbb.shape),
        ],
        out_specs=(
            tiled_spec,
            pl.BlockSpec((1, tm, sbw), row),
            tiled_spec,
            pl.BlockSpec((1, tm, d), row),
            pl.BlockSpec((1, tm, d), row),
        ),
        out_shape=out_shape,
        scratch_shapes=[pltpu.VMEM((tm + SUBLANES, cwid), F32)],
        compiler_params=pltpu.CompilerParams(
            dimension_semantics=("arbitrary", "arbitrary"),
            vmem_limit_bytes=VMEM_LIMIT),
        name="mix_in",
    )(x, g, win, cw, wbb)


SB_STAGES = 5
SB_STAGE_GAP = 1
SB_SLOTS = SB_STAGE_GAP * (SB_STAGES - 1)


def _sb_attn_kernel(qt_ref, k_ref, vt_ref, tri_ref, o_ref,
                    qm_ref, c_ref, acc_ref, z_buf, p_buf, cum_buf, w_buf):
    nq, tq = qt_ref.shape[2], qt_ref.shape[4]
    tk = SB_TILE
    hd = SB_HEAD_DIM
    dummy = nq

    def prep(i, carry):
        qt = qt_ref[0, 0, i]
        zero = jnp.zeros((hd, tq), BF16)
        qm_ref[0, i, 0:hd, :] = qt[0:hd]
        qm_ref[0, i, hd:2 * hd, :] = zero
        qm_ref[1, i, 0:hd, :] = zero
        qm_ref[1, i, hd:2 * hd, :] = qt[hd:2 * hd]
        return carry

    lax.fori_loop(0, nq, prep, 0)
    qm_ref[:, dummy] = jnp.zeros((2, 2 * hd, tq), BF16)
    @pl.when(jnp.logical_and(pl.program_id(0) == 0, pl.program_id(1) == 0))
    def _():
        z_buf[...] = jnp.zeros(z_buf.shape, F32)
        p_buf[...] = jnp.zeros(p_buf.shape, BF16)
        cum_buf[...] = jnp.zeros(cum_buf.shape, F32)
        w_buf[...] = jnp.zeros(w_buf.shape, BF16)

    key = lax.broadcasted_iota(jnp.int32, (tk, tq), 0)
    qry = lax.broadcasted_iota(jnp.int32, (tk, tq), 1)
    causal = key < qry

    def stage_ma(i, j, h, slot):
        kt = k_ref[0, pl.ds(pl.multiple_of(j * tk, tk), tk), :]
        z_buf[slot] = _dot(kt, qm_ref[h, i])

    def softplus2(z):
        p = jnp.maximum(z, 0.0) + jnp.log(1.0 + jnp.exp2(-jnp.abs(z))) * LOG2E
        return p.astype(BF16)

    def stage_ea(slot, masked):
        z = z_buf[slot]
        if masked:
            z = jnp.where(causal, z, MASKED)
            z_buf[slot] = z
        p_buf[slot] = softplus2(z)

    def stage_mb(slot):
        cum_buf[slot] = _dot(tri_ref[...], p_buf[slot])

    def stage_eb(i, h, slot, masked):
        cum = cum_buf[slot]
        if masked:
            w_buf[slot] = jnp.exp2(z_buf[slot] - cum).astype(BF16)
            c_ref[h, i] = jnp.broadcast_to(-cum[0:1, :], (SUBLANES, tq))
        else:
            c = c_ref[h, i, 0:1, :]
            w_buf[slot] = jnp.exp2(z_buf[slot] - cum + c).astype(BF16)
            c_ref[h, i] = jnp.broadcast_to(c - cum[0:1, :], (SUBLANES, tq))

    def stage_mc(i, j, h, slot, masked):
        vt = vt_ref[0, 0, j, h * hd:(h + 1) * hd, :]
        out = _dot(vt, w_buf[slot])
        if masked:
            acc_ref[h, i] = out
        else:
            acc_ref[h, i] += out

    def block(item, diagonal):
        def body(u, carry):
            for r in range(SB_SLOTS):
                s = SB_SLOTS * u + r

                def in_stage(k):
                    back = k * SB_STAGE_GAP
                    return item(s - back), (r - back) % 2, (r - back) % SB_SLOTS

                (i4, j4), h4, slot4 = in_stage(4)
                stage_mc(i4, j4, h4, slot4, diagonal(4, r))
                (i3, _), h3, slot3 = in_stage(3)
                stage_eb(i3, h3, slot3, diagonal(3, r))
                stage_mb(in_stage(2)[2])
                stage_ea(in_stage(1)[2], diagonal(1, r))
                (i0, j0), h0, slot0 = in_stage(0)
                stage_ma(i0, j0, h0, slot0)
            return carry
        return body

    drain = SB_STAGE_GAP * (SB_STAGES - 1)

    def blocks_for(steps):
        return (steps + SB_SLOTS - 1) // SB_SLOTS

    def run_head():
        n_diag, n_next = 2 * nq, 2 * (nq - 1)

        def item(m):
            t0, t1 = m // 2, (m - n_diag) // 2
            in_diag = jnp.logical_and(m >= 0, m < n_diag)
            in_next = jnp.logical_and(m >= n_diag, m < n_diag + n_next)
            i = jnp.where(in_diag, t0, jnp.where(in_next, t1 + 1, dummy))
            j = jnp.where(in_diag, t0, jnp.where(in_next, t1, 0))
            return i, j

        turn = n_diag // SB_SLOTS
        lax.fori_loop(0, turn, block(item, lambda k, r: True), 0)
        block(item, lambda k, r: r < k * SB_STAGE_GAP)(turn, 0)
        lax.fori_loop(turn + 1, blocks_for(n_diag + n_next + drain),
                      block(item, lambda k, r: False), 0)

    def run_distance(d):
        n_items = 2 * (nq - d)

        def item(m):
            valid = jnp.logical_and(m >= 0, m < n_items)
            t = m // 2
            return jnp.where(valid, d + t, dummy), jnp.where(valid, t, 0)

        lax.fori_loop(0, blocks_for(n_items + drain), block(item, lambda k, r: False), 0)

    def alive_after(d):
        cc = jnp.maximum(c_ref[0, 0:nq], c_ref[1, 0:nq])
        blk = lax.broadcasted_iota(jnp.int32, cc.shape, 0)
        top = jnp.max(jnp.where(blk > d, cc, MASKED), axis=0)
        return (jnp.max(top) > DEAD_LOG2).astype(jnp.int32)

    run_head()

    def more(state):
        d, alive = state
        return jnp.logical_and(d < nq, alive > 0)

    def step(state):
        d, _ = state
        run_distance(d)
        return d + 1, alive_after(d)

    lax.while_loop(more, step, (jnp.int32(2), alive_after(1)))

    for h in range(2):
        o_ref[0, 0, :, h * hd:(h + 1) * hd, :] = acc_ref[h, 0:nq].astype(BF16)


def _sb_attn(qt, k, vt, tri):
    b, pairs, nq, _, tq = qt.shape
    s, sbw = k.shape[1], k.shape[2]
    tk = SB_TILE
    assert nq >= 2 and (2 * nq) % SB_SLOTS == 0 and 2 * nq - 2 > SB_STAGE_GAP * (SB_STAGES - 1)
    return pl.pallas_call(
        _sb_attn_kernel,
        grid=(b, pairs),
        in_specs=[
            pl.BlockSpec((1, 1, nq, LANES, tq), lambda bi, p: (bi, p, 0, 0, 0)),
            pl.BlockSpec((1, s, LANES), lambda bi, p: (bi, 0, p)),
            pl.BlockSpec((1, 1, s // tk, LANES, tk), lambda bi, p: (bi, p, 0, 0, 0)),
            _const_spec(tri.shape),
        ],
        out_specs=pl.BlockSpec((1, 1, nq, LANES, tq), lambda bi, p: (bi, p, 0, 0, 0)),
        out_shape=jax.ShapeDtypeStruct(qt.shape, BF16),
        scratch_shapes=[
            pltpu.VMEM((2, nq + 1, LANES, tq), BF16),
            pltpu.VMEM((2, nq + 1, SUBLANES, tq), F32),
            pltpu.VMEM((2, nq + 1, SB_HEAD_DIM, tq), F32),
            pltpu.VMEM((SB_SLOTS, tk, tq), F32),
            pltpu.VMEM((SB_SLOTS, tk, tq), BF16),
            pltpu.VMEM((SB_SLOTS, tk, tq), F32),
            pltpu.VMEM((SB_SLOTS, tk, tq), BF16),
        ],
        compiler_params=pltpu.CompilerParams(
            dimension_semantics=("arbitrary", "arbitrary"),
            vmem_limit_bytes=VMEM_LIMIT),
        name="sb_attn",
    )(qt, k, vt, tri)


def _mem_kv_kernel(mem_ref, g_ref, wkv_ref, kt_ref, v_ref):
    nb, m, d = mem_ref.shape
    hm = _rms(mem_ref[...].reshape(nb * m, d), g_ref[...]).astype(BF16)
    kt = _dot_tn(_bf16(wkv_ref[:, 0:d]), hm).astype(BF16)
    v = _dot(hm, _bf16(wkv_ref[:, d:2 * d])).astype(BF16)
    dh = kt_ref.shape[2]
    for bi in range(nb):
        for hd in range(kt_ref.shape[1]):
            kt_ref[bi, hd] = kt[hd * dh:(hd + 1) * dh, bi * m:(bi + 1) * m]
        v_ref[bi] = v[bi * m:(bi + 1) * m]


def _mem_kv(mem, g, wkv):
    b, m, d = mem.shape
    dh = d // MEM_HEADS
    kt_shape, v_shape = (b, MEM_HEADS, dh, m), (b, m, d)
    return pl.pallas_call(
        _mem_kv_kernel,
        grid=(1,),
        in_specs=[_const_spec(mem.shape), _const_spec(g.shape), _const_spec(wkv.shape)],
        out_specs=(pl.BlockSpec(kt_shape, lambda i: (0, 0, 0, 0)),
                   pl.BlockSpec(v_shape, lambda i: (0, 0, 0))),
        out_shape=(jax.ShapeDtypeStruct(kt_shape, BF16), jax.ShapeDtypeStruct(v_shape, BF16)),
        compiler_params=pltpu.CompilerParams(
            dimension_semantics=("arbitrary",), vmem_limit_bytes=VMEM_LIMIT),
        name="mem_kv",
    )(mem, g, wkv)


def _mix_out_kernel(x_ref, oa_ref, sa_ref, mb_ref, wba_ref, wmo_ref, gq_ref, wmq_ref,
                    kmt_ref, vm_ref, wmemo_ref, o_ref):
    oat = jnp.concatenate(
        [jnp.concatenate([oa_ref[0, p, j] for j in range(oa_ref.shape[2])], axis=1)
         for p in range(oa_ref.shape[1])], axis=0)
    branch_a = lax.dot_general(oat, _bf16(wba_ref[...]), (((0,), (0,)), ((), ())),
                               preferred_element_type=F32)
    merged =(sa_ref[0].astype(F32) * branch_a + mb_ref[0].astype(F32)).astype(BF16)
    x1 = x_ref[0] + _dot(merged, _bf16(wmo_ref[...]))

    dh = kmt_ref.shape[2]
    xg = (x1 * gq_ref[...]).astype(BF16)
    qm = (_dot(xg, _bf16(wmq_ref[...])) * (_rms_scale(x1) * (dh ** -0.5))).astype(BF16)
    cols = [slice(hd * dh, (hd + 1) * dh) for hd in range(MEM_HEADS)]
    scores = [_dot(qm[:, c], kmt_ref[0, hd]) for hd, c in enumerate(cols)]
    heads = []
    for sc, c in zip(scores, cols):
        e = jnp.exp(sc - jnp.max(sc, axis=-1, keepdims=True))
        denom = jnp.sum(e, axis=-1, keepdims=True)
        o = _dot(e.astype(BF16), vm_ref[0, :, c]) / denom
        heads.append(o.astype(BF16))
    o_ref[0] = x1 + _dot(jnp.concatenate(heads, axis=1), _bf16(wmemo_ref[...]))


def _mix_out(x, oa, sa, mb, wba, wmo, gq, wmq, kmt, vm, wmemo):
    b, s, d = x.shape
    tm = MIX_TOKEN_TILE
    row = lambda bi, i: (bi, i, 0)
    return pl.pallas_call(
        _mix_out_kernel,
        grid=(b, s // tm),
        in_specs=[
            pl.BlockSpec((1, tm, d), row),
            pl.BlockSpec((1, oa.shape[1], tm // oa.shape[4], LANES, oa.shape[4]),
                         lambda bi, i: (bi, 0, i, 0, 0)),
            pl.BlockSpec((1, tm, d), row),
            pl.BlockSpec((1, tm, d), row),
            _const_spec(wba.shape), _const_spec(wmo.shape), _const_spec(gq.shape),
            _const_spec(wmq.shape),
            pl.BlockSpec((1,) + kmt.shape[1:], lambda bi, i: (bi, 0, 0, 0)),
            pl.BlockSpec((1,) + vm.shape[1:], lambda bi, i: (bi, 0, 0)),
            _const_spec(wmemo.shape),
        ],
        out_specs=pl.BlockSpec((1, tm, d), row),
        out_shape=jax.ShapeDtypeStruct((b, s, d), F32),
        compiler_params=pltpu.CompilerParams(
            dimension_semantics=("parallel", "parallel"), vmem_limit_bytes=VMEM_LIMIT),
        name="mix_out",
    )(x, oa, sa, mb, wba, wmo, gq, wmq, kmt, vm, wmemo)


def _ffn_kernel(x_ref, gf_ref, win_ref, wo_ref, gfin_ref, o_ref, *, chunks, final_norm):
    hidden = wo_ref.shape[0]
    tiles = pl.cdiv(hidden, MXU_DIM)
    bounds = [min(hidden, (c * tiles // chunks) * MXU_DIM) for c in range(chunks + 1)]
    x = x_ref[0]
    xg = (x * gf_ref[...]).astype(BF16)
    r = _rms_scale(x)
    acts = []
    for c in range(chunks):
        lo, hi = bounds[c], bounds[c + 1]
        gate = _dot(xg, _bf16(win_ref[:, lo:hi])) * r
        up = _dot(xg, _bf16(win_ref[:, hidden + lo:hidden + hi])) * r
        acts.append((gate * jax.nn.sigmoid(gate) * up).astype(BF16))
    y = x
    for c in range(chunks):
        y = y + _dot(acts[c], _bf16(wo_ref[bounds[c]:bounds[c + 1], :]))
    o_ref[0] = _rms(y, gfin_ref[...]) if final_norm else y


def _ffn(x, gf, win, wo, gfin, final_norm):
    b, s, d = x.shape
    tm = FFN_TOKEN_TILE
    row = lambda bi, i: (bi, i, 0)
    chunks = 4
    assert win.shape[1] == 2 * wo.shape[0] and wo.shape[0] % LANES == 0
    return pl.pallas_call(
        functools.partial(_ffn_kernel, chunks=chunks, final_norm=final_norm),
        grid=(b, s // tm),
        in_specs=[pl.BlockSpec((1, tm, d), row), _const_spec(gf.shape), _const_spec(win.shape),
                  _const_spec(wo.shape), _const_spec(gfin.shape)],
        out_specs=pl.BlockSpec((1, tm, d), row),
        out_shape=jax.ShapeDtypeStruct((b, s, d), F32),
        compiler_params=pltpu.CompilerParams(
            dimension_semantics=("parallel", "parallel"), vmem_limit_bytes=VMEM_LIMIT),
        name="ffn",
    )(x, gf, win, wo, gfin)


def kernel(x, mem, norm_mix, w_in, conv_w, w_branch_a, w_branch_b, w_mix_out, norm_mem_q,
           norm_mem_kv, w_mem_q, w_mem_kv, w_mem_o, norm_ffn, w_ffn_in, w_ffn_out, norm_final):
    depth = w_in.shape[0]
    assert x.shape[1] % MIX_TOKEN_TILE == 0 and x.shape[1] % FFN_TOKEN_TILE == 0
    assert MIX_TOKEN_TILE % SB_TILE == 0
    assert LANES == 2 * SB_HEAD_DIM

    idx = jnp.arange(SB_TILE)
    tri = (idx[None, :] >= idx[:, None]).astype(BF16)

    for l in range(depth):
        qt, k, vt, sa, mb = _mix_in(x, norm_mix[l][None, :], w_in[l], conv_w[l], w_branch_b[l])
        oa = _sb_attn(qt, k, vt, tri)
        kmt, vm = _mem_kv(mem, norm_mem_kv[l][None, :], w_mem_kv[l])
        x = _mix_out(x, oa, sa, mb, w_branch_a[l], w_mix_out[l], norm_mem_q[l][None, :],
                     w_mem_q[l], kmt, vm, w_mem_o[l])
        x = _ffn(x, norm_ffn[l][None, :], w_ffn_in[l], w_ffn_out[l],
                 norm_final[None, :], final_norm=(l == depth - 1))
    return x
```

```python
import functools
import math

import jax
import jax.numpy as jnp
from jax import lax
from jax.experimental import pallas as pl
from jax.experimental.pallas import tpu as pltpu

EPS = 1e-6
SB_HEADS = 8
SB_HEAD_DIM = 64
CONV_K = 3
MEM_HEADS = 4

LANES = 128
SUBLANES = 8
MXU_DIM = 256
FFN_TOKEN_TILE = 1024
MIX_TOKEN_TILE = 1024
SB_TILE = 256
MIB = 1024 * 1024
VMEM_LIMIT = 60 * MIB

LOG2E = math.log2(math.e)
MASKED = -1e30
DEAD_LOG2 = -160.0

BF16 = jnp.bfloat16
F32 = jnp.float32


def _dot(a, b):
    return jnp.dot(a, b, preferred_element_type=F32)


def _dot_tn(a, b):
    return lax.dot_general(a, b, (((0,), (1,)), ((), ())), preferred_element_type=F32)


def _bf16(w):
    return w.astype(BF16)


def _rms_scale(x):
    return lax.rsqrt(jnp.mean(x * x, axis=-1, keepdims=True) + EPS)


def _rms(x, g):
    return x * _rms_scale(x) * g


def _const_spec(shape):
    zeros = (0,) * len(shape)
    return pl.BlockSpec(shape, lambda *_: zeros, pipeline_mode=pl.Buffered(1))


def _mix_in_kernel(x_ref, g_ref, win_ref, cw_ref, wbb_ref,
                   qt_ref, k_ref, vt_ref, sa_ref, mb_ref, cu_buf):
    tm = x_ref.shape[1]
    d = x_ref.shape[2]
    cw = wbb_ref.shape[0]
    sbw = k_ref.shape[2]
    halo = SUBLANES
    o_conv = 3 * sbw
    o_ga = o_conv + 3 * cw
    o_gb = o_ga + d

    @pl.when(pl.program_id(1) == 0)
    def _():
        cu_buf[0:halo, :] = jnp.zeros((halo, cw), F32)

    h = _rms(x_ref[0], g_ref[...]).astype(BF16)

    ugc = _dot(h, _bf16(win_ref[:, o_conv:o_ga]))
    cu = ugc[:, 2 * cw:3 * cw] * ugc[:, 0:cw]
    gate_b = ugc[:, cw:2 * cw]
    cu_buf[halo:halo + tm, :] = cu
    conv = cw_ref[2:3, :] * cu
    for i in range(CONV_K - 1):
        shift = CONV_K - 1 - i
        conv = conv + cw_ref[i:i + 1, :] * cu_buf[halo - shift:halo - shift + tm, :]
    cu_buf[0:halo, :] = cu[tm - halo:tm, :]
    yb = (gate_b * conv).astype(BF16)

    sa_ref[0] = jax.nn.sigmoid(_dot(h, _bf16(win_ref[:, o_ga:o_gb]))).astype(BF16)
    sig_b = jax.nn.sigmoid(_dot(h, _bf16(win_ref[:, o_gb:o_gb + d])))
    mb_ref[0] = (sig_b * _dot(yb, _bf16(wbb_ref[...]))).astype(BF16)

    qt = _dot_tn(_bf16(win_ref[:, 0:sbw]), h) * (LOG2E * SB_HEAD_DIM ** -0.5)
    qt = qt.astype(BF16)
    vt = _dot_tn(_bf16(win_ref[:, 2 * sbw:3 * sbw]), h).astype(BF16)
    for p in range(qt_ref.shape[1]):
        for j in range(qt_ref.shape[2]):
            rows, cols = slice(p * LANES, (p + 1) * LANES), slice(j * SB_TILE, (j + 1) * SB_TILE)
            qt_ref[0, p, j] = qt[rows, cols]
            vt_ref[0, p, j] = vt[rows, cols]
    k_ref[0] = _dot(h, _bf16(win_ref[:, sbw:2 * sbw])).astype(BF16)


def _mix_in(x, g, win, cw, wbb):
    b, s, d = x.shape
    tm = MIX_TOKEN_TILE
    sbw = SB_HEADS * SB_HEAD_DIM
    cwid = wbb.shape[0]
    assert win.shape[1] == 3 * sbw + 3 * cwid + 2 * d
    pairs = sbw // LANES
    tiled = jax.ShapeDtypeStruct((b, pairs, s // SB_TILE, LANES, SB_TILE), BF16)
    out_shape = (
        tiled,
        jax.ShapeDtypeStruct((b, s, sbw), BF16),
        tiled,
        jax.ShapeDtypeStruct((b, s, d), BF16),
        jax.ShapeDtypeStruct((b, s, d), BF16),
    )
    row = lambda bi, i: (bi, i, 0)
    tiled_spec = pl.BlockSpec((1, pairs, tm // SB_TILE, LANES, SB_TILE),
                              lambda bi, i: (bi, 0, i, 0, 0))
    return pl.pallas_call(
        _mix_in_kernel,
        grid=(b, s // tm),
        in_specs=[
            pl.BlockSpec((1, tm, d), row),
            _const_spec(g.shape), _const_spec(win.shape), _const_spec(cw.shape),
            _const_spec(wbb.shape),
        ],
        out_specs=(
            tiled_spec,
            pl.BlockSpec((1, tm, sbw), row),
            tiled_spec,
            pl.BlockSpec((1, tm, d), row),
            pl.BlockSpec((1, tm, d), row),
        ),
        out_shape=out_shape,
        scratch_shapes=[pltpu.VMEM((tm + SUBLANES, cwid), F32)],
        compiler_params=pltpu.CompilerParams(
            dimension_semantics=("arbitrary", "arbitrary"),
            vmem_limit_bytes=VMEM_LIMIT),
        name="mix_in",
    )(x, g, win, cw, wbb)


SB_STAGES = 5
SB_DRAIN = SB_STAGES - 1
SB_SLOTS = 4
SB_LONG_BLOCK = 8
SB_SHORT_BLOCK = 4


def _sb_attn_kernel(qt_ref, k_ref, vt_ref, tri_ref, o_ref,
                    qm_ref, c_ref, acc_ref, z_buf, p_buf, cum_buf, w_buf):
    nq, tq = qt_ref.shape[2], qt_ref.shape[4]
    tk = SB_TILE
    hd = SB_HEAD_DIM
    dummy = nq

    def prep(i, carry):
        qt = qt_ref[0, 0, i]
        zero = jnp.zeros((hd, tq), BF16)
        qm_ref[0, i, 0:hd, :] = qt[0:hd]
        qm_ref[0, i, hd:2 * hd, :] = zero
        qm_ref[1, i, 0:hd, :] = zero
        qm_ref[1, i, hd:2 * hd, :] = qt[hd:2 * hd]
        return carry

    lax.fori_loop(0, nq, prep, 0)
    qm_ref[:, dummy] = jnp.zeros((2, 2 * hd, tq), BF16)
    @pl.when(jnp.logical_and(pl.program_id(0) == 0, pl.program_id(1) == 0))
    def _():
        z_buf[...] = jnp.zeros(z_buf.shape, F32)
        p_buf[...] = jnp.zeros(p_buf.shape, BF16)
        cum_buf[...] = jnp.zeros(cum_buf.shape, F32)
        w_buf[...] = jnp.zeros(w_buf.shape, BF16)

    key = lax.broadcasted_iota(jnp.int32, (tk, tq), 0)
    qry = lax.broadcasted_iota(jnp.int32, (tk, tq), 1)
    causal = key < qry

    def stage_ma(i, j, h, slot):
        kt = k_ref[0, pl.ds(pl.multiple_of(j * tk, tk), tk), :]
        z_buf[slot] = _dot(kt, qm_ref[h, i])

    def softplus2(z):
        p = jnp.maximum(z, 0.0) + jnp.log(1.0 + jnp.exp2(-jnp.abs(z))) * LOG2E
        return p.astype(BF16)

    def stage_ea(slot, masked):
        z = z_buf[slot]
        if masked:
            z = jnp.where(causal, z, MASKED)
            z_buf[slot] = z
        p_buf[slot] = softplus2(z)

    def stage_mb(slot):
        cum_buf[slot] = _dot(tri_ref[...], p_buf[slot])

    def stage_eb(i, h, slot, masked):
        cum = cum_buf[slot]
        if masked:
            w_buf[slot] = jnp.exp2(z_buf[slot] - cum).astype(BF16)
            c_ref[h, i] = jnp.broadcast_to(-cum[0:1, :], (SUBLANES, tq))
        else:
            c = c_ref[h, i, 0:1, :]
            w_buf[slot] = jnp.exp2(z_buf[slot] - cum + c).astype(BF16)
            c_ref[h, i] = jnp.broadcast_to(c - cum[0:1, :], (SUBLANES, tq))

    def stage_mc(i, j, h, slot, masked):
        vt = vt_ref[0, 0, j, h * hd:(h + 1) * hd, :]
        out = _dot(vt, w_buf[slot])
        if masked:
            acc_ref[h, i] = out
        else:
            acc_ref[h, i] += out

    def block(item, diagonal, size, first_step=0):
        def body(u, carry):
            for r in range(size):
                s = first_step + size * u + r

                def in_stage(k):
                    return item(s - k), (r - k) % 2, (r - k) % SB_SLOTS

                (i4, j4), h4, slot4 = in_stage(4)
                stage_mc(i4, j4, h4, slot4, diagonal(4, r))
                (i3, _), h3, slot3 = in_stage(3)
                stage_eb(i3, h3, slot3, diagonal(3, r))
                stage_mb(in_stage(2)[2])
                stage_ea(in_stage(1)[2], diagonal(1, r))
                (i0, j0), h0, slot0 = in_stage(0)
                stage_ma(i0, j0, h0, slot0)
            return carry
        return body

    def run_plain(item, first_step, last_step):
        never = lambda k, r: False
        n_long = (last_step - first_step) // SB_LONG_BLOCK
        lax.fori_loop(0, n_long, block(item, never, SB_LONG_BLOCK, first_step), 0)
        rest = first_step + SB_LONG_BLOCK * n_long
        n_short = (last_step - rest + SB_SHORT_BLOCK - 1) // SB_SHORT_BLOCK
        lax.fori_loop(0, n_short, block(item, never, SB_SHORT_BLOCK, rest), 0)

    def run_head():
        n_diag, n_next = 2 * nq, 2 * (nq - 1)

        def item(m):
            t0, t1 = m // 2, (m - n_diag) // 2
            in_diag = jnp.logical_and(m >= 0, m < n_diag)
            in_next = jnp.logical_and(m >= n_diag, m < n_diag + n_next)
            i = jnp.where(in_diag, t0, jnp.where(in_next, t1 + 1, dummy))
            j = jnp.where(in_diag, t0, jnp.where(in_next, t1, 0))
            return i, j

        lax.fori_loop(0, n_diag // SB_LONG_BLOCK,
                      block(item, lambda k, r: True, SB_LONG_BLOCK), 0)
        block(item, lambda k, r: r < k, SB_LONG_BLOCK, n_diag)(0, 0)
        run_plain(item, n_diag + SB_LONG_BLOCK, n_diag + n_next + SB_DRAIN)

    def run_distance(d):
        n_items = 2 * (nq - d)

        def item(m):
            valid = jnp.logical_and(m >= 0, m < n_items)
            t = m // 2
            return jnp.where(valid, d + t, dummy), jnp.where(valid, t, 0)

        run_plain(item, 0, n_items + SB_DRAIN)

    def alive_after(d):
        cc = jnp.maximum(c_ref[0, 0:nq], c_ref[1, 0:nq])
        blk = lax.broadcasted_iota(jnp.int32, cc.shape, 0)
        top = jnp.max(jnp.where(blk > d, cc, MASKED), axis=0)
        return (jnp.max(top) > DEAD_LOG2).astype(jnp.int32)

    run_head()

    def more(state):
        d, alive = state
        return jnp.logical_and(d < nq, alive > 0)

    def step(state):
        d, _ = state
        run_distance(d)
        return d + 1, alive_after(d)

    lax.while_loop(more, step, (jnp.int32(2), alive_after(1)))

    for h in range(2):
        o_ref[0, 0, :, h * hd:(h + 1) * hd, :] = acc_ref[h, 0:nq].astype(BF16)


def _sb_attn(qt, k, vt, tri):
    b, pairs, nq, _, tq = qt.shape
    s, sbw = k.shape[1], k.shape[2]
    tk = SB_TILE
    assert (2 * nq) % SB_LONG_BLOCK == 0 and 2 * nq - 2 > SB_DRAIN <= SB_LONG_BLOCK
    assert 2 * (nq - 1) + SB_DRAIN >= SB_LONG_BLOCK
    return pl.pallas_call(
        _sb_attn_kernel,
        grid=(b, pairs),
        in_specs=[
            pl.BlockSpec((1, 1, nq, LANES, tq), lambda bi, p: (bi, p, 0, 0, 0)),
            pl.BlockSpec((1, s, LANES), lambda bi, p: (bi, 0, p)),
            pl.BlockSpec((1, 1, s // tk, LANES, tk), lambda bi, p: (bi, p, 0, 0, 0)),
            _const_spec(tri.shape),
        ],
        out_specs=pl.BlockSpec((1, 1, nq, LANES, tq), lambda bi, p: (bi, p, 0, 0, 0)),
        out_shape=jax.ShapeDtypeStruct(qt.shape, BF16),
        scratch_shapes=[
            pltpu.VMEM((2, nq + 1, LANES, tq), BF16),
            pltpu.VMEM((2, nq + 1, SUBLANES, tq), F32),
            pltpu.VMEM((2, nq + 1, SB_HEAD_DIM, tq), F32),
            pltpu.VMEM((SB_SLOTS, tk, tq), F32),
            pltpu.VMEM((SB_SLOTS, tk, tq), BF16),
            pltpu.VMEM((SB_SLOTS, tk, tq), F32),
            pltpu.VMEM((SB_SLOTS, tk, tq), BF16),
        ],
        compiler_params=pltpu.CompilerParams(
            dimension_semantics=("arbitrary", "arbitrary"),
            vmem_limit_bytes=VMEM_LIMIT),
        name="sb_attn",
    )(qt, k, vt, tri)


def _mem_kv_kernel(mem_ref, g_ref, wkv_ref, kt_ref, v_ref):
    nb, m, d = mem_ref.shape
    hm = _rms(mem_ref[...].reshape(nb * m, d), g_ref[...]).astype(BF16)
    kt = _dot_tn(_bf16(wkv_ref[:, 0:d]), hm).astype(BF16)
    v = _dot(hm, _bf16(wkv_ref[:, d:2 * d])).astype(BF16)
    dh = kt_ref.shape[2]
    for bi in range(nb):
        for hd in range(kt_ref.shape[1]):
            kt_ref[bi, hd] = kt[hd * dh:(hd + 1) * dh, bi * m:(bi + 1) * m]
        v_ref[bi] = v[bi * m:(bi + 1) * m]


def _mem_kv(mem, g, wkv):
    b, m, d = mem.shape
    dh = d // MEM_HEADS
    kt_shape, v_shape = (b, MEM_HEADS, dh, m), (b, m, d)
    return pl.pallas_call(
        _mem_kv_kernel,
        grid=(1,),
        in_specs=[_const_spec(mem.shape), _const_spec(g.shape), _const_spec(wkv.shape)],
        out_specs=(pl.BlockSpec(kt_shape, lambda i: (0, 0, 0, 0)),
                   pl.BlockSpec(v_shape, lambda i: (0, 0, 0))),
        out_shape=(jax.ShapeDtypeStruct(kt_shape, BF16), jax.ShapeDtypeStruct(v_shape, BF16)),
        compiler_params=pltpu.CompilerParams(
            dimension_semantics=("arbitrary",), vmem_limit_bytes=VMEM_LIMIT),
        name="mem_kv",
    )(mem, g, wkv)


def _mix_out_kernel(x_ref, oa_ref, sa_ref, mb_ref, wba_ref, wmo_ref, gq_ref, wmq_ref,
                    kmt_ref, vm_ref, wmemo_ref, o_ref):
    oat = jnp.concatenate(
        [jnp.concatenate([oa_ref[0, p, j] for j in range(oa_ref.shape[2])], axis=1)
         for p in range(oa_ref.shape[1])], axis=0)
    branch_a = lax.dot_general(oat, _bf16(wba_ref[...]), (((0,), (0,)), ((), ())),
                               preferred_element_type=F32)
    merged =(sa_ref[0].astype(F32) * branch_a + mb_ref[0].astype(F32)).astype(BF16)
    x1 = x_ref[0] + _dot(merged, _bf16(wmo_ref[...]))

    dh = kmt_ref.shape[2]
    xg = (x1 * gq_ref[...]).astype(BF16)
    qm = (_dot(xg, _bf16(wmq_ref[...])) * (_rms_scale(x1) * (dh ** -0.5))).astype(BF16)
    cols = [slice(hd * dh, (hd + 1) * dh) for hd in range(MEM_HEADS)]
    scores = [_dot(qm[:, c], kmt_ref[0, hd]) for hd, c in enumerate(cols)]
    heads = []
    for sc, c in zip(scores, cols):
        e = jnp.exp(sc - jnp.max(sc, axis=-1, keepdims=True))
        denom = jnp.sum(e, axis=-1, keepdims=True)
        o = _dot(e.astype(BF16), vm_ref[0, :, c]) / denom
        heads.append(o.astype(BF16))
    o_ref[0] = x1 + _dot(jnp.concatenate(heads, axis=1), _bf16(wmemo_ref[...]))


def _mix_out(x, oa, sa, mb, wba, wmo, gq, wmq, kmt, vm, wmemo):
    b, s, d = x.shape
    tm = MIX_TOKEN_TILE
    row = lambda bi, i: (bi, i, 0)
    return pl.pallas_call(
        _mix_out_kernel,
        grid=(b, s // tm),
        in_specs=[
            pl.BlockSpec((1, tm, d), row),
            pl.BlockSpec((1, oa.shape[1], tm // oa.shape[4], LANES, oa.shape[4]),
                         lambda bi, i: (bi, 0, i, 0, 0)),
            pl.BlockSpec((1, tm, d), row),
            pl.BlockSpec((1, tm, d), row),
            _const_spec(wba.shape), _const_spec(wmo.shape), _const_spec(gq.shape),
            _const_spec(wmq.shape),
            pl.BlockSpec((1,) + kmt.shape[1:], lambda bi, i: (bi, 0, 0, 0)),
            pl.BlockSpec((1,) + vm.shape[1:], lambda bi, i: (bi, 0, 0)),
            _const_spec(wmemo.shape),
        ],
        out_specs=pl.BlockSpec((1, tm, d), row),
        out_shape=jax.ShapeDtypeStruct((b, s, d), F32),
        compiler_params=pltpu.CompilerParams(
            dimension_semantics=("parallel", "parallel"), vmem_limit_bytes=VMEM_LIMIT),
        name="mix_out",
    )(x, oa, sa, mb, wba, wmo, gq, wmq, kmt, vm, wmemo)


def _ffn_kernel(x_ref, gf_ref, win_ref, wo_ref, gfin_ref, o_ref, *, chunks, final_norm):
    hidden = wo_ref.shape[0]
    tiles = pl.cdiv(hidden, MXU_DIM)
    bounds = [min(hidden, (c * tiles // chunks) * MXU_DIM) for c in range(chunks + 1)]
    x = x_ref[0]
    xg = (x * gf_ref[...]).astype(BF16)
    r = _rms_scale(x)
    acts = []
    for c in range(chunks):
        lo, hi = bounds[c], bounds[c + 1]
        gate = _dot(xg, _bf16(win_ref[:, lo:hi])) * r
        up = _dot(xg, _bf16(win_ref[:, hidden + lo:hidden + hi])) * r
        acts.append((gate * jax.nn.sigmoid(gate) * up).astype(BF16))
    y = x
    for c in range(chunks):
        y = y + _dot(acts[c], _bf16(wo_ref[bounds[c]:bounds[c + 1], :]))
    o_ref[0] = _rms(y, gfin_ref[...]) if final_norm else y


def _ffn(x, gf, win, wo, gfin, final_norm):
    b, s, d = x.shape
    tm = FFN_TOKEN_TILE
    row = lambda bi, i: (bi, i, 0)
    chunks = 4
    assert win.shape[1] == 2 * wo.shape[0] and wo.shape[0] % LANES == 0
    return pl.pallas_call(
        functools.partial(_ffn_kernel, chunks=chunks, final_norm=final_norm),
        grid=(b, s // tm),
        in_specs=[pl.BlockSpec((1, tm, d), row), _const_spec(gf.shape), _const_spec(win.shape),
                  _const_spec(wo.shape), _const_spec(gfin.shape)],
        out_specs=pl.BlockSpec((1, tm, d), row),
        out_shape=jax.ShapeDtypeStruct((b, s, d), F32),
        compiler_params=pltpu.CompilerParams(
            dimension_semantics=("parallel", "parallel"), vmem_limit_bytes=VMEM_LIMIT),
        name="ffn",
    )(x, gf, win, wo, gfin)


def kernel(x, mem, norm_mix, w_in, conv_w, w_branch_a, w_branch_b, w_mix_out, norm_mem_q,
           norm_mem_kv, w_mem_q, w_mem_kv, w_mem_o, norm_ffn, w_ffn_in, w_ffn_out, norm_final):
    depth = w_in.shape[0]
    assert x.shape[1] % MIX_TOKEN_TILE == 0 and x.shape[1] % FFN_TOKEN_TILE == 0
    assert MIX_TOKEN_TILE % SB_TILE == 0
    assert LANES == 2 * SB_HEAD_DIM

    idx = jnp.arange(SB_TILE)
    tri = (idx[None, :] >= idx[:, None]).astype(BF16)

    for l in range(depth):
        qt, k, vt, sa, mb = _mix_in(x, norm_mix[l][None, :], w_in[l], conv_w[l], w_branch_b[l])
        oa = _sb_attn(qt, k, vt, tri)
        kmt, vm = _mem_kv(mem, norm_mem_kv[l][None, :], w_mem_kv[l])
        x = _mix_out(x, oa, sa, mb, w_branch_a[l], w_mix_out[l], norm_mem_q[l][None, :],
                     w_mem_q[l], kmt, vm, w_mem_o[l])
        x = _ffn(x, norm_ffn[l][None, :], w_ffn_in[l], w_ffn_out[l],
                 norm_final[None, :], final_norm=(l == depth - 1))
    return x
```

```python
import functools
import math

import jax
import jax.numpy as jnp
from jax import lax
from jax.experimental import pallas as pl
from jax.experimental.pallas import tpu as pltpu

EPS = 1e-6
SB_HEADS = 8
SB_HEAD_DIM = 64
CONV_K = 3
MEM_HEADS = 4

LANES = 128
SUBLANES = 8
MXU_DIM = 256
FFN_TOKEN_TILE = 1024
MIX_TOKEN_TILE = 1024
SB_TILE = 256
MIB = 1024 * 1024
VMEM_LIMIT = 60 * MIB

LOG2E = math.log2(math.e)
MASKED = -1e30
DEAD_LOG2 = -160.0

BF16 = jnp.bfloat16
F32 = jnp.float32


def _dot(a, b):
    return jnp.dot(a, b, preferred_element_type=F32)


def _dot_tn(a, b):
    return lax.dot_general(a, b, (((0,), (1,)), ((), ())), preferred_element_type=F32)


def _bf16(w):
    return w.astype(BF16)


def _rms_scale(x):
    return lax.rsqrt(jnp.mean(x * x, axis=-1, keepdims=True) + EPS)


def _rms(x, g):
    return x * _rms_scale(x) * g


def _const_spec(shape):
    zeros = (0,) * len(shape)
    return pl.BlockSpec(shape, lambda *_: zeros, pipeline_mode=pl.Buffered(1))


def _mix_in_kernel(x_ref, g_ref, win_ref, cw_ref, wbb_ref,
                   qt_ref, k_ref, vt_ref, sa_ref, mb_ref, cu_buf):
    tm = x_ref.shape[1]
    d = x_ref.shape[2]
    cw = wbb_ref.shape[0]
    sbw = k_ref.shape[2]
    halo = SUBLANES
    o_conv = 3 * sbw
    o_ga = o_conv + 3 * cw
    o_gb = o_ga + d

    @pl.when(pl.program_id(1) == 0)
    def _():
        cu_buf[0:halo, :] = jnp.zeros((halo, cw), F32)

    h = _rms(x_ref[0], g_ref[...]).astype(BF16)

    ugc = _dot(h, _bf16(win_ref[:, o_conv:o_ga]))
    cu = ugc[:, 2 * cw:3 * cw] * ugc[:, 0:cw]
    gate_b = ugc[:, cw:2 * cw]
    cu_buf[halo:halo + tm, :] = cu
    conv = cw_ref[2:3, :] * cu
    for i in range(CONV_K - 1):
        shift = CONV_K - 1 - i
        conv = conv + cw_ref[i:i + 1, :] * cu_buf[halo - shift:halo - shift + tm, :]
    cu_buf[0:halo, :] = cu[tm - halo:tm, :]
    yb = (gate_b * conv).astype(BF16)

    sa_ref[0] = jax.nn.sigmoid(_dot(h, _bf16(win_ref[:, o_ga:o_gb]))).astype(BF16)
    sig_b = jax.nn.sigmoid(_dot(h, _bf16(win_ref[:, o_gb:o_gb + d])))
    mb_ref[0] = (sig_b * _dot(yb, _bf16(wbb_ref[...]))).astype(BF16)

    qt = _dot_tn(_bf16(win_ref[:, 0:sbw]), h) * (LOG2E * SB_HEAD_DIM ** -0.5)
    qt = qt.astype(BF16)
    vt = _dot_tn(_bf16(win_ref[:, 2 * sbw:3 * sbw]), h).astype(BF16)
    for p in range(qt_ref.shape[1]):
        for j in range(qt_ref.shape[2]):
            rows, cols = slice(p * LANES, (p + 1) * LANES), slice(j * SB_TILE, (j + 1) * SB_TILE)
            qt_ref[0, p, j] = qt[rows, cols]
            vt_ref[0, p, j] = vt[rows, cols]
    k_ref[0] = _dot(h, _bf16(win_ref[:, sbw:2 * sbw])).astype(BF16)


def _mix_in(x, g, win, cw, wbb):
    b, s, d = x.shape
    tm = MIX_TOKEN_TILE
    sbw = SB_HEADS * SB_HEAD_DIM
    cwid = wbb.shape[0]
    assert win.shape[1] == 3 * sbw + 3 * cwid + 2 * d
    pairs = sbw // LANES
    tiled = jax.ShapeDtypeStruct((b, pairs, s // SB_TILE, LANES, SB_TILE), BF16)
    out_shape = (
        tiled,
        jax.ShapeDtypeStruct((b, s, sbw), BF16),
        tiled,
        jax.ShapeDtypeStruct((b, s, d), BF16),
        jax.ShapeDtypeStruct((b, s, d), BF16),
    )
    row = lambda bi, i: (bi, i, 0)
    tiled_spec = pl.BlockSpec((1, pairs, tm // SB_TILE, LANES, SB_TILE),
                              lambda bi, i: (bi, 0, i, 0, 0))
    return pl.pallas_call(
        _mix_in_kernel,
        grid=(b, s // tm),
        in_specs=[
            pl.BlockSpec((1, tm, d), row),
            _const_spec(g.shape), _const_spec(win.shape), _const_spec(cw.shape),
            _const_spec(wbb.shape),
        ],
        out_specs=(
            tiled_spec,
            pl.BlockSpec((1, tm, sbw), row),
            tiled_spec,
            pl.BlockSpec((1, tm, d), row),
            pl.BlockSpec((1, tm, d), row),
        ),
        out_shape=out_shape,
        scratch_shapes=[pltpu.VMEM((tm + SUBLANES, cwid), F32)],
        compiler_params=pltpu.CompilerParams(
            dimension_semantics=("arbitrary", "arbitrary"),
            vmem_limit_bytes=VMEM_LIMIT),
        name="mix_in",
    )(x, g, win, cw, wbb)


SB_STAGES = 5
SB_DRAIN = SB_STAGES - 1
SB_SLOTS = 4
SB_LONG_BLOCK = 16
SB_SHORT_BLOCK = 4


def _sb_attn_kernel(qt_ref, k_ref, vt_ref, tri_ref, o_ref,
                    qm_ref, c_ref, acc_ref, z_buf, p_buf, cum_buf, w_buf):
    nq, tq = qt_ref.shape[2], qt_ref.shape[4]
    tk = SB_TILE
    hd = SB_HEAD_DIM
    dummy = nq

    def prep(i, carry):
        qt = qt_ref[0, 0, i]
        zero = jnp.zeros((hd, tq), BF16)
        qm_ref[0, i, 0:hd, :] = qt[0:hd]
        qm_ref[0, i, hd:2 * hd, :] = zero
        qm_ref[1, i, 0:hd, :] = zero
        qm_ref[1, i, hd:2 * hd, :] = qt[hd:2 * hd]
        return carry

    lax.fori_loop(0, nq, prep, 0)
    qm_ref[:, dummy] = jnp.zeros((2, 2 * hd, tq), BF16)
    @pl.when(jnp.logical_and(pl.program_id(0) == 0, pl.program_id(1) == 0))
    def _():
        z_buf[...] = jnp.zeros(z_buf.shape, F32)
        p_buf[...] = jnp.zeros(p_buf.shape, BF16)
        cum_buf[...] = jnp.zeros(cum_buf.shape, F32)
        w_buf[...] = jnp.zeros(w_buf.shape, BF16)

    key = lax.broadcasted_iota(jnp.int32, (tk, tq), 0)
    qry = lax.broadcasted_iota(jnp.int32, (tk, tq), 1)
    causal = key < qry

    def stage_ma(i, j, h, slot):
        kt = k_ref[0, pl.ds(pl.multiple_of(j * tk, tk), tk), :]
        z_buf[slot] = _dot(kt, qm_ref[h, i])

    def softplus2(z):
        p = jnp.maximum(z, 0.0) + jnp.log(1.0 + jnp.exp2(-jnp.abs(z))) * LOG2E
        return p.astype(BF16)

    def stage_ea(slot, masked):
        z = z_buf[slot]
        if masked:
            z = jnp.where(causal, z, MASKED)
            z_buf[slot] = z
        p_buf[slot] = softplus2(z)

    def stage_mb(slot):
        cum_buf[slot] = _dot(tri_ref[...], p_buf[slot])

    def stage_eb(i, h, slot, masked):
        cum = cum_buf[slot]
        if masked:
            w_buf[slot] = jnp.exp2(z_buf[slot] - cum).astype(BF16)
            c_ref[h, i] = jnp.broadcast_to(-cum[0:1, :], (SUBLANES, tq))
        else:
            c = c_ref[h, i, 0:1, :]
            w_buf[slot] = jnp.exp2(z_buf[slot] - cum + c).astype(BF16)
            c_ref[h, i] = jnp.broadcast_to(c - cum[0:1, :], (SUBLANES, tq))

    def stage_mc(i, j, h, slot, masked):
        vt = vt_ref[0, 0, j, h * hd:(h + 1) * hd, :]
        out = _dot(vt, w_buf[slot])
        if masked:
            acc_ref[h, i] = out
        else:
            acc_ref[h, i] += out

    def block(item, diagonal, size, first_step=0):
        def body(u, carry):
            for r in range(size):
                s = first_step + size * u + r

                def in_stage(k):
                    return item(s - k), (r - k) % 2, (r - k) % SB_SLOTS

                (i4, j4), h4, slot4 = in_stage(4)
                stage_mc(i4, j4, h4, slot4, diagonal(4, r))
                (i3, _), h3, slot3 = in_stage(3)
                stage_eb(i3, h3, slot3, diagonal(3, r))
                stage_mb(in_stage(2)[2])
                stage_ea(in_stage(1)[2], diagonal(1, r))
                (i0, j0), h0, slot0 = in_stage(0)
                stage_ma(i0, j0, h0, slot0)
            return carry
        return body

    def run_plain(item, first_step, last_step):
        never = lambda k, r: False
        n_long = (last_step - first_step) // SB_LONG_BLOCK
        lax.fori_loop(0, n_long, block(item, never, SB_LONG_BLOCK, first_step), 0)
        rest = first_step + SB_LONG_BLOCK * n_long
        n_short = (last_step - rest + SB_SHORT_BLOCK - 1) // SB_SHORT_BLOCK
        lax.fori_loop(0, n_short, block(item, never, SB_SHORT_BLOCK, rest), 0)

    def run_head():
        n_diag, n_next = 2 * nq, 2 * (nq - 1)

        def item(m):
            t0, t1 = m // 2, (m - n_diag) // 2
            in_diag = jnp.logical_and(m >= 0, m < n_diag)
            in_next = jnp.logical_and(m >= n_diag, m < n_diag + n_next)
            i = jnp.where(in_diag, t0, jnp.where(in_next, t1 + 1, dummy))
            j = jnp.where(in_diag, t0, jnp.where(in_next, t1, 0))
            return i, j

        lax.fori_loop(0, n_diag // SB_LONG_BLOCK,
                      block(item, lambda k, r: True, SB_LONG_BLOCK), 0)
        block(item, lambda k, r: r < k, SB_LONG_BLOCK, n_diag)(0, 0)
        run_plain(item, n_diag + SB_LONG_BLOCK, n_diag + n_next + SB_DRAIN)

    def run_distance(d):
        n_items = 2 * (nq - d)

        def item(m):
            valid = jnp.logical_and(m >= 0, m < n_items)
            t = m // 2
            return jnp.where(valid, d + t, dummy), jnp.where(valid, t, 0)

        run_plain(item, 0, n_items + SB_DRAIN)

    def alive_after(d):
        cc = jnp.maximum(c_ref[0, 0:nq], c_ref[1, 0:nq])
        blk = lax.broadcasted_iota(jnp.int32, cc.shape, 0)
        top = jnp.max(jnp.where(blk > d, cc, MASKED), axis=0)
        return (jnp.max(top) > DEAD_LOG2).astype(jnp.int32)

    run_head()

    def more(state):
        d, alive = state
        return jnp.logical_and(d < nq, alive > 0)

    def step(state):
        d, _ = state
        run_distance(d)
        return d + 1, alive_after(d)

    lax.while_loop(more, step, (jnp.int32(2), alive_after(1)))

    for h in range(2):
        o_ref[0, 0, :, h * hd:(h + 1) * hd, :] = acc_ref[h, 0:nq].astype(BF16)


def _sb_attn(qt, k, vt, tri):
    b, pairs, nq, _, tq = qt.shape
    s, sbw = k.shape[1], k.shape[2]
    tk = SB_TILE
    assert (2 * nq) % SB_LONG_BLOCK == 0 and 2 * nq - 2 > SB_DRAIN <= SB_LONG_BLOCK
    assert 2 * (nq - 1) + SB_DRAIN >= SB_LONG_BLOCK
    return pl.pallas_call(
        _sb_attn_kernel,
        grid=(b, pairs),
        in_specs=[
            pl.BlockSpec((1, 1, nq, LANES, tq), lambda bi, p: (bi, p, 0, 0, 0)),
            pl.BlockSpec((1, s, LANES), lambda bi, p: (bi, 0, p)),
            pl.BlockSpec((1, 1, s // tk, LANES, tk), lambda bi, p: (bi, p, 0, 0, 0)),
            _const_spec(tri.shape),
        ],
        out_specs=pl.BlockSpec((1, 1, nq, LANES, tq), lambda bi, p: (bi, p, 0, 0, 0)),
        out_shape=jax.ShapeDtypeStruct(qt.shape, BF16),
        scratch_shapes=[
            pltpu.VMEM((2, nq + 1, LANES, tq), BF16),
            pltpu.VMEM((2, nq + 1, SUBLANES, tq), F32),
            pltpu.VMEM((2, nq + 1, SB_HEAD_DIM, tq), F32),
            pltpu.VMEM((SB_SLOTS, tk, tq), F32),
            pltpu.VMEM((SB_SLOTS, tk, tq), BF16),
            pltpu.VMEM((SB_SLOTS, tk, tq), F32),
            pltpu.VMEM((SB_SLOTS, tk, tq), BF16),
        ],
        compiler_params=pltpu.CompilerParams(
            dimension_semantics=("arbitrary", "arbitrary"),
            vmem_limit_bytes=VMEM_LIMIT),
        name="sb_attn",
    )(qt, k, vt, tri)


def _mem_kv_kernel(mem_ref, g_ref, wkv_ref, kt_ref, v_ref):
    nb, m, d = mem_ref.shape
    hm = _rms(mem_ref[...].reshape(nb * m, d), g_ref[...]).astype(BF16)
    kt = _dot_tn(_bf16(wkv_ref[:, 0:d]), hm).astype(BF16)
    v = _dot(hm, _bf16(wkv_ref[:, d:2 * d])).astype(BF16)
    dh = kt_ref.shape[2]
    for bi in range(nb):
        for hd in range(kt_ref.shape[1]):
            kt_ref[bi, hd] = kt[hd * dh:(hd + 1) * dh, bi * m:(bi + 1) * m]
        v_ref[bi] = v[bi * m:(bi + 1) * m]


def _mem_kv(mem, g, wkv):
    b, m, d = mem.shape
    dh = d // MEM_HEADS
    kt_shape, v_shape = (b, MEM_HEADS, dh, m), (b, m, d)
    return pl.pallas_call(
        _mem_kv_kernel,
        grid=(1,),
        in_specs=[_const_spec(mem.shape), _const_spec(g.shape), _const_spec(wkv.shape)],
        out_specs=(pl.BlockSpec(kt_shape, lambda i: (0, 0, 0, 0)),
                   pl.BlockSpec(v_shape, lambda i: (0, 0, 0))),
        out_shape=(jax.ShapeDtypeStruct(kt_shape, BF16), jax.ShapeDtypeStruct(v_shape, BF16)),
        compiler_params=pltpu.CompilerParams(
            dimension_semantics=("arbitrary",), vmem_limit_bytes=VMEM_LIMIT),
        name="mem_kv",
    )(mem, g, wkv)


def _mix_out_kernel(x_ref, oa_ref, sa_ref, mb_ref, wba_ref, wmo_ref, gq_ref, wmq_ref,
                    kmt_ref, vm_ref, wmemo_ref, o_ref):
    oat = jnp.concatenate(
        [jnp.concatenate([oa_ref[0, p, j] for j in range(oa_ref.shape[2])], axis=1)
         for p in range(oa_ref.shape[1])], axis=0)
    branch_a = lax.dot_general(oat, _bf16(wba_ref[...]), (((0,), (0,)), ((), ())),
                               preferred_element_type=F32)
    merged =(sa_ref[0].astype(F32) * branch_a + mb_ref[0].astype(F32)).astype(BF16)
    x1 = x_ref[0] + _dot(merged, _bf16(wmo_ref[...]))

    dh = kmt_ref.shape[2]
    xg = (x1 * gq_ref[...]).astype(BF16)
    qm = (_dot(xg, _bf16(wmq_ref[...])) * (_rms_scale(x1) * (dh ** -0.5))).astype(BF16)
    cols = [slice(hd * dh, (hd + 1) * dh) for hd in range(MEM_HEADS)]
    scores = [_dot(qm[:, c], kmt_ref[0, hd]) for hd, c in enumerate(cols)]
    heads = []
    for sc, c in zip(scores, cols):
        e = jnp.exp(sc - jnp.max(sc, axis=-1, keepdims=True))
        denom = jnp.sum(e, axis=-1, keepdims=True)
        o = _dot(e.astype(BF16), vm_ref[0, :, c]) / denom
        heads.append(o.astype(BF16))
    o_ref[0] = x1 + _dot(jnp.concatenate(heads, axis=1), _bf16(wmemo_ref[...]))


def _mix_out(x, oa, sa, mb, wba, wmo, gq, wmq, kmt, vm, wmemo):
    b, s, d = x.shape
    tm = MIX_TOKEN_TILE
    row = lambda bi, i: (bi, i, 0)
    return pl.pallas_call(
        _mix_out_kernel,
        grid=(b, s // tm),
        in_specs=[
            pl.BlockSpec((1, tm, d), row),
            pl.BlockSpec((1, oa.shape[1], tm // oa.shape[4], LANES, oa.shape[4]),
                         lambda bi, i: (bi, 0, i, 0, 0)),
            pl.BlockSpec((1, tm, d), row),
            pl.BlockSpec((1, tm, d), row),
            _const_spec(wba.shape), _const_spec(wmo.shape), _const_spec(gq.shape),
            _const_spec(wmq.shape),
            pl.BlockSpec((1,) + kmt.shape[1:], lambda bi, i: (bi, 0, 0, 0)),
            pl.BlockSpec((1,) + vm.shape[1:], lambda bi, i: (bi, 0, 0)),
            _const_spec(wmemo.shape),
        ],
        out_specs=pl.BlockSpec((1, tm, d), row),
        out_shape=jax.ShapeDtypeStruct((b, s, d), F32),
        compiler_params=pltpu.CompilerParams(
            dimension_semantics=("parallel", "parallel"), vmem_limit_bytes=VMEM_LIMIT),
        name="mix_out",
    )(x, oa, sa, mb, wba, wmo, gq, wmq, kmt, vm, wmemo)


def _ffn_kernel(x_ref, gf_ref, win_ref, wo_ref, gfin_ref, o_ref, *, chunks, final_norm):
    hidden = wo_ref.shape[0]
    tiles = pl.cdiv(hidden, MXU_DIM)
    bounds = [min(hidden, (c * tiles // chunks) * MXU_DIM) for c in range(chunks + 1)]
    x = x_ref[0]
    xg = (x * gf_ref[...]).astype(BF16)
    r = _rms_scale(x)
    acts = []
    for c in range(chunks):
        lo, hi = bounds[c], bounds[c + 1]
        gate = _dot(xg, _bf16(win_ref[:, lo:hi])) * r
        up = _dot(xg, _bf16(win_ref[:, hidden + lo:hidden + hi])) * r
        acts.append((gate * jax.nn.sigmoid(gate) * up).astype(BF16))
    y = x
    for c in range(chunks):
        y = y + _dot(acts[c], _bf16(wo_ref[bounds[c]:bounds[c + 1], :]))
    o_ref[0] = _rms(y, gfin_ref[...]) if final_norm else y


def _ffn(x, gf, win, wo, gfin, final_norm):
    b, s, d = x.shape
    tm = FFN_TOKEN_TILE
    row = lambda bi, i: (bi, i, 0)
    chunks = 4
    assert win.shape[1] == 2 * wo.shape[0] and wo.shape[0] % LANES == 0
    return pl.pallas_call(
        functools.partial(_ffn_kernel, chunks=chunks, final_norm=final_norm),
        grid=(b, s // tm),
        in_specs=[pl.BlockSpec((1, tm, d), row), _const_spec(gf.shape), _const_spec(win.shape),
                  _const_spec(wo.shape), _const_spec(gfin.shape)],
        out_specs=pl.BlockSpec((1, tm, d), row),
        out_shape=jax.ShapeDtypeStruct((b, s, d), F32),
        compiler_params=pltpu.CompilerParams(
            dimension_semantics=("parallel", "parallel"), vmem_limit_bytes=VMEM_LIMIT),
        name="ffn",
    )(x, gf, win, wo, gfin)


def kernel(x, mem, norm_mix, w_in, conv_w, w_branch_a, w_branch_b, w_mix_out, norm_mem_q,
           norm_mem_kv, w_mem_q, w_mem_kv, w_mem_o, norm_ffn, w_ffn_in, w_ffn_out, norm_final):
    depth = w_in.shape[0]
    assert x.shape[1] % MIX_TOKEN_TILE == 0 and x.shape[1] % FFN_TOKEN_TILE == 0
    assert MIX_TOKEN_TILE % SB_TILE == 0
    assert LANES == 2 * SB_HEAD_DIM

    idx = jnp.arange(SB_TILE)
    tri = (idx[None, :] >= idx[:, None]).astype(BF16)

    for l in range(depth):
        qt, k, vt, sa, mb = _mix_in(x, norm_mix[l][None, :], w_in[l], conv_w[l], w_branch_b[l])
        oa = _sb_attn(qt, k, vt, tri)
        kmt, vm = _mem_kv(mem, norm_mem_kv[l][None, :], w_mem_kv[l])
        x = _mix_out(x, oa, sa, mb, w_branch_a[l], w_mix_out[l], norm_mem_q[l][None, :],
                     w_mem_q[l], kmt, vm, w_mem_o[l])
        x = _ffn(x, norm_ffn[l][None, :], w_ffn_in[l], w_ffn_out[l],
                 norm_final[None, :], final_norm=(l == depth - 1))
    return x
```

```python
import functools
import math

import jax
import jax.numpy as jnp
from jax import lax
from jax.experimental import pallas as pl
from jax.experimental.pallas import tpu as pltpu

EPS = 1e-6
SB_HEADS = 8
SB_HEAD_DIM = 64
CONV_K = 3
MEM_HEADS = 4

LANES = 128
SUBLANES = 8
MXU_DIM = 256
FFN_TOKEN_TILE = 1024
FFN_TAIL_GROUPS = 4
MIX_TOKEN_TILE = 1024
SB_TILE = 256
MIB = 1024 * 1024
VMEM_LIMIT = 60 * MIB

LOG2E = math.log2(math.e)
MASKED = -1e30
DEAD_LOG2 = -160.0

BF16 = jnp.bfloat16
F32 = jnp.float32


def _dot(a, b):
    return jnp.dot(a, b, preferred_element_type=F32)


def _dot_tn(a, b):
    return lax.dot_general(a, b, (((0,), (1,)), ((), ())), preferred_element_type=F32)


def _bf16(w):
    return w.astype(BF16)


def _rms_scale(x):
    return lax.rsqrt(jnp.mean(x * x, axis=-1, keepdims=True) + EPS)


def _rms(x, g):
    return x * _rms_scale(x) * g


def _const_spec(shape):
    zeros = (0,) * len(shape)
    return pl.BlockSpec(shape, lambda *_: zeros, pipeline_mode=pl.Buffered(1))


def _mix_in_kernel(x_ref, g_ref, win_ref, cw_ref, wbb_ref,
                   qt_ref, k_ref, vt_ref, sa_ref, mb_ref, cu_buf):
    tm = x_ref.shape[1]
    d = x_ref.shape[2]
    cw = wbb_ref.shape[0]
    sbw = k_ref.shape[2]
    halo = SUBLANES
    o_conv = 3 * sbw
    o_ga = o_conv + 3 * cw
    o_gb = o_ga + d

    @pl.when(pl.program_id(1) == 0)
    def _():
        cu_buf[0:halo, :] = jnp.zeros((halo, cw), F32)

    h = _rms(x_ref[0], g_ref[...]).astype(BF16)

    ugc = _dot(h, _bf16(win_ref[:, o_conv:o_ga]))
    cu = ugc[:, 2 * cw:3 * cw] * ugc[:, 0:cw]
    gate_b = ugc[:, cw:2 * cw]
    cu_buf[halo:halo + tm, :] = cu
    conv = cw_ref[2:3, :] * cu
    for i in range(CONV_K - 1):
        shift = CONV_K - 1 - i
        conv = conv + cw_ref[i:i + 1, :] * cu_buf[halo - shift:halo - shift + tm, :]
    cu_buf[0:halo, :] = cu[tm - halo:tm, :]
    yb = (gate_b * conv).astype(BF16)

    sa_ref[0] = jax.nn.sigmoid(_dot(h, _bf16(win_ref[:, o_ga:o_gb]))).astype(BF16)
    sig_b = jax.nn.sigmoid(_dot(h, _bf16(win_ref[:, o_gb:o_gb + d])))
    mb_ref[0] = (sig_b * _dot(yb, _bf16(wbb_ref[...]))).astype(BF16)

    qt = _dot_tn(_bf16(win_ref[:, 0:sbw]), h) * (LOG2E * SB_HEAD_DIM ** -0.5)
    qt = qt.astype(BF16)
    vt = _dot_tn(_bf16(win_ref[:, 2 * sbw:3 * sbw]), h).astype(BF16)
    for p in range(qt_ref.shape[1]):
        for j in range(qt_ref.shape[2]):
            rows, cols = slice(p * LANES, (p + 1) * LANES), slice(j * SB_TILE, (j + 1) * SB_TILE)
            qt_ref[0, p, j] = qt[rows, cols]
            vt_ref[0, p, j] = vt[rows, cols]
    k_ref[0] = _dot(h, _bf16(win_ref[:, sbw:2 * sbw])).astype(BF16)


def _mix_in(x, g, win, cw, wbb):
    b, s, d = x.shape
    tm = MIX_TOKEN_TILE
    sbw = SB_HEADS * SB_HEAD_DIM
    cwid = wbb.shape[0]
    assert win.shape[1] == 3 * sbw + 3 * cwid + 2 * d
    pairs = sbw // LANES
    tiled = jax.ShapeDtypeStruct((b, pairs, s // SB_TILE, LANES, SB_TILE), BF16)
    out_shape = (
        tiled,
        jax.ShapeDtypeStruct((b, s, sbw), BF16),
        tiled,
        jax.ShapeDtypeStruct((b, s, d), BF16),
        jax.ShapeDtypeStruct((b, s, d), BF16),
    )
    row = lambda bi, i: (bi, i, 0)
    tiled_spec = pl.BlockSpec((1, pairs, tm // SB_TILE, LANES, SB_TILE),
                              lambda bi, i: (bi, 0, i, 0, 0))
    return pl.pallas_call(
        _mix_in_kernel,
        grid=(b, s // tm),
        in_specs=[
            pl.BlockSpec((1, tm, d), row),
            _const_spec(g.shape), _const_spec(win.shape), _const_spec(cw.shape),
            _const_spec(wbb.shape),
        ],
        out_specs=(
            tiled_spec,
            pl.BlockSpec((1, tm, sbw), row),
            tiled_spec,
            pl.BlockSpec((1, tm, d), row),
            pl.BlockSpec((1, tm, d), row),
        ),
        out_shape=out_shape,
        scratch_shapes=[pltpu.VMEM((tm + SUBLANES, cwid), F32)],
        compiler_params=pltpu.CompilerParams(
            dimension_semantics=("arbitrary", "arbitrary"),
            vmem_limit_bytes=VMEM_LIMIT),
        name="mix_in",
    )(x, g, win, cw, wbb)


SB_STAGES = 5
SB_DRAIN = SB_STAGES - 1
SB_SLOTS = 4
SB_LONG_BLOCK = 16
SB_SHORT_BLOCK = 4


def _sb_attn_kernel(qt_ref, k_ref, vt_ref, tri_ref, o_ref,
                    qm_ref, c_ref, acc_ref, z_buf, p_buf, cum_buf, w_buf):
    nq, tq = qt_ref.shape[2], qt_ref.shape[4]
    tk = SB_TILE
    hd = SB_HEAD_DIM
    dummy = nq

    def prep(i, carry):
        qt = qt_ref[0, 0, i]
        zero = jnp.zeros((hd, tq), BF16)
        qm_ref[0, i, 0:hd, :] = qt[0:hd]
        qm_ref[0, i, hd:2 * hd, :] = zero
        qm_ref[1, i, 0:hd, :] = zero
        qm_ref[1, i, hd:2 * hd, :] = qt[hd:2 * hd]
        return carry

    lax.fori_loop(0, nq, prep, 0)
    qm_ref[:, dummy] = jnp.zeros((2, 2 * hd, tq), BF16)
    @pl.when(jnp.logical_and(pl.program_id(0) == 0, pl.program_id(1) == 0))
    def _():
        z_buf[...] = jnp.zeros(z_buf.shape, F32)
        p_buf[...] = jnp.zeros(p_buf.shape, BF16)
        cum_buf[...] = jnp.zeros(cum_buf.shape, F32)
        w_buf[...] = jnp.zeros(w_buf.shape, BF16)

    key = lax.broadcasted_iota(jnp.int32, (tk, tq), 0)
    qry = lax.broadcasted_iota(jnp.int32, (tk, tq), 1)
    causal = key < qry

    def stage_ma(i, j, h, slot):
        kt = k_ref[0, pl.ds(pl.multiple_of(j * tk, tk), tk), :]
        z_buf[slot] = _dot(kt, qm_ref[h, i])

    def softplus2(z):
        p = jnp.maximum(z, 0.0) + jnp.log(1.0 + jnp.exp2(-jnp.abs(z))) * LOG2E
        return p.astype(BF16)

    def stage_ea(slot, masked):
        z = z_buf[slot]
        if masked:
            z = jnp.where(causal, z, MASKED)
            z_buf[slot] = z
        p_buf[slot] = softplus2(z)

    def stage_mb(slot):
        cum_buf[slot] = _dot(tri_ref[...], p_buf[slot])

    def stage_eb(i, h, slot, masked):
        cum = cum_buf[slot]
        if masked:
            w_buf[slot] = jnp.exp2(z_buf[slot] - cum).astype(BF16)
            c_ref[h, i] = jnp.broadcast_to(-cum[0:1, :], (SUBLANES, tq))
        else:
            c = c_ref[h, i, 0:1, :]
            w_buf[slot] = jnp.exp2(z_buf[slot] - cum + c).astype(BF16)
            c_ref[h, i] = jnp.broadcast_to(c - cum[0:1, :], (SUBLANES, tq))

    def stage_mc(i, j, h, slot, masked):
        vt = vt_ref[0, 0, j, h * hd:(h + 1) * hd, :]
        out = _dot(vt, w_buf[slot])
        if masked:
            acc_ref[h, i] = out
        else:
            acc_ref[h, i] += out

    def block(item, diagonal, size, first_step=0):
        def body(u, carry):
            for r in range(size):
                s = first_step + size * u + r

                def in_stage(k):
                    return item(s - k), (r - k) % 2, (r - k) % SB_SLOTS

                (i4, j4), h4, slot4 = in_stage(4)
                stage_mc(i4, j4, h4, slot4, diagonal(4, r))
                (i3, _), h3, slot3 = in_stage(3)
                stage_eb(i3, h3, slot3, diagonal(3, r))
                stage_mb(in_stage(2)[2])
                stage_ea(in_stage(1)[2], diagonal(1, r))
                (i0, j0), h0, slot0 = in_stage(0)
                stage_ma(i0, j0, h0, slot0)
            return carry
        return body

    def run_plain(item, first_step, last_step):
        never = lambda k, r: False
        n_long = (last_step - first_step) // SB_LONG_BLOCK
        lax.fori_loop(0, n_long, block(item, never, SB_LONG_BLOCK, first_step), 0)
        rest = first_step + SB_LONG_BLOCK * n_long
        n_short = (last_step - rest + SB_SHORT_BLOCK - 1) // SB_SHORT_BLOCK
        lax.fori_loop(0, n_short, block(item, never, SB_SHORT_BLOCK, rest), 0)

    def run_head():
        n_diag, n_next = 2 * nq, 2 * (nq - 1)

        def item(m):
            t0, t1 = m // 2, (m - n_diag) // 2
            in_diag = jnp.logical_and(m >= 0, m < n_diag)
            in_next = jnp.logical_and(m >= n_diag, m < n_diag + n_next)
            i = jnp.where(in_diag, t0, jnp.where(in_next, t1 + 1, dummy))
            j = jnp.where(in_diag, t0, jnp.where(in_next, t1, 0))
            return i, j

        lax.fori_loop(0, n_diag // SB_LONG_BLOCK,
                      block(item, lambda k, r: True, SB_LONG_BLOCK), 0)
        block(item, lambda k, r: r < k, SB_LONG_BLOCK, n_diag)(0, 0)
        run_plain(item, n_diag + SB_LONG_BLOCK, n_diag + n_next + SB_DRAIN)

    def run_distance(d):
        n_items = 2 * (nq - d)

        def item(m):
            valid = jnp.logical_and(m >= 0, m < n_items)
            t = m // 2
            return jnp.where(valid, d + t, dummy), jnp.where(valid, t, 0)

        run_plain(item, 0, n_items + SB_DRAIN)

    def alive_after(d):
        cc = jnp.maximum(c_ref[0, 0:nq], c_ref[1, 0:nq])
        blk = lax.broadcasted_iota(jnp.int32, cc.shape, 0)
        top = jnp.max(jnp.where(blk > d, cc, MASKED), axis=0)
        return (jnp.max(top) > DEAD_LOG2).astype(jnp.int32)

    run_head()

    def more(state):
        d, alive = state
        return jnp.logical_and(d < nq, alive > 0)

    def step(state):
        d, _ = state
        run_distance(d)
        return d + 1, alive_after(d)

    lax.while_loop(more, step, (jnp.int32(2), alive_after(1)))

    for h in range(2):
        o_ref[0, 0, :, h * hd:(h + 1) * hd, :] = acc_ref[h, 0:nq].astype(BF16)


def _sb_attn(qt, k, vt, tri):
    b, pairs, nq, _, tq = qt.shape
    s, sbw = k.shape[1], k.shape[2]
    tk = SB_TILE
    assert (2 * nq) % SB_LONG_BLOCK == 0 and 2 * nq - 2 > SB_DRAIN <= SB_LONG_BLOCK
    assert 2 * (nq - 1) + SB_DRAIN >= SB_LONG_BLOCK
    return pl.pallas_call(
        _sb_attn_kernel,
        grid=(b, pairs),
        in_specs=[
            pl.BlockSpec((1, 1, nq, LANES, tq), lambda bi, p: (bi, p, 0, 0, 0)),
            pl.BlockSpec((1, s, LANES), lambda bi, p: (bi, 0, p)),
            pl.BlockSpec((1, 1, s // tk, LANES, tk), lambda bi, p: (bi, p, 0, 0, 0)),
            _const_spec(tri.shape),
        ],
        out_specs=pl.BlockSpec((1, 1, nq, LANES, tq), lambda bi, p: (bi, p, 0, 0, 0)),
        out_shape=jax.ShapeDtypeStruct(qt.shape, BF16),
        scratch_shapes=[
            pltpu.VMEM((2, nq + 1, LANES, tq), BF16),
            pltpu.VMEM((2, nq + 1, SUBLANES, tq), F32),
            pltpu.VMEM((2, nq + 1, SB_HEAD_DIM, tq), F32),
            pltpu.VMEM((SB_SLOTS, tk, tq), F32),
            pltpu.VMEM((SB_SLOTS, tk, tq), BF16),
            pltpu.VMEM((SB_SLOTS, tk, tq), F32),
            pltpu.VMEM((SB_SLOTS, tk, tq), BF16),
        ],
        compiler_params=pltpu.CompilerParams(
            dimension_semantics=("arbitrary", "arbitrary"),
            vmem_limit_bytes=VMEM_LIMIT),
        name="sb_attn",
    )(qt, k, vt, tri)


def _mem_kv_kernel(mem_ref, g_ref, wkv_ref, kt_ref, v_ref):
    nb, m, d = mem_ref.shape
    hm = _rms(mem_ref[...].reshape(nb * m, d), g_ref[...]).astype(BF16)
    kt = _dot_tn(_bf16(wkv_ref[:, 0:d]), hm).astype(BF16)
    v = _dot(hm, _bf16(wkv_ref[:, d:2 * d])).astype(BF16)
    dh = kt_ref.shape[2]
    for bi in range(nb):
        for hd in range(kt_ref.shape[1]):
            kt_ref[bi, hd] = kt[hd * dh:(hd + 1) * dh, bi * m:(bi + 1) * m]
        v_ref[bi] = v[bi * m:(bi + 1) * m]


def _mem_kv(mem, g, wkv):
    b, m, d = mem.shape
    dh = d // MEM_HEADS
    kt_shape, v_shape = (b, MEM_HEADS, dh, m), (b, m, d)
    return pl.pallas_call(
        _mem_kv_kernel,
        grid=(1,),
        in_specs=[_const_spec(mem.shape), _const_spec(g.shape), _const_spec(wkv.shape)],
        out_specs=(pl.BlockSpec(kt_shape, lambda i: (0, 0, 0, 0)),
                   pl.BlockSpec(v_shape, lambda i: (0, 0, 0))),
        out_shape=(jax.ShapeDtypeStruct(kt_shape, BF16), jax.ShapeDtypeStruct(v_shape, BF16)),
        compiler_params=pltpu.CompilerParams(
            dimension_semantics=("arbitrary",), vmem_limit_bytes=VMEM_LIMIT),
        name="mem_kv",
    )(mem, g, wkv)


def _mix_out_kernel(x_ref, oa_ref, sa_ref, mb_ref, wba_ref, wmo_ref, gq_ref, wmq_ref,
                    kmt_ref, vm_ref, wmemo_ref, o_ref):
    w_a = _bf16(wba_ref[...])
    branch_a = jnp.concatenate(
        [lax.dot_general(
            jnp.concatenate([oa_ref[0, p, j] for p in range(oa_ref.shape[1])], axis=0),
            w_a, (((0,), (0,)), ((), ())), preferred_element_type=F32)
         for j in range(oa_ref.shape[2])], axis=0)
    merged = (sa_ref[0].astype(F32) * branch_a + mb_ref[0].astype(F32)).astype(BF16)
    x1 = x_ref[0] + _dot(merged, _bf16(wmo_ref[...]))

    dh = kmt_ref.shape[2]
    xg = (x1 * gq_ref[...]).astype(BF16)
    qm = (_dot(xg, _bf16(wmq_ref[...])) * (_rms_scale(x1) * (dh ** -0.5))).astype(BF16)
    cols = [slice(hd * dh, (hd + 1) * dh) for hd in range(MEM_HEADS)]
    scores = [_dot(qm[:, c], kmt_ref[0, hd]) for hd, c in enumerate(cols)]
    heads = []
    for sc, c in zip(scores, cols):
        e = jnp.exp(sc - jnp.max(sc, axis=-1, keepdims=True))
        denom = jnp.sum(e, axis=-1, keepdims=True)
        o = _dot(e.astype(BF16), vm_ref[0, :, c]) / denom
        heads.append(o.astype(BF16))
    o_ref[0] = x1 + _dot(jnp.concatenate(heads, axis=1), _bf16(wmemo_ref[...]))


def _mix_out(x, oa, sa, mb, wba, wmo, gq, wmq, kmt, vm, wmemo):
    b, s, d = x.shape
    tm = MIX_TOKEN_TILE
    row = lambda bi, i: (bi, i, 0)
    return pl.pallas_call(
        _mix_out_kernel,
        grid=(b, s // tm),
        in_specs=[
            pl.BlockSpec((1, tm, d), row),
            pl.BlockSpec((1, oa.shape[1], tm // oa.shape[4], LANES, oa.shape[4]),
                         lambda bi, i: (bi, 0, i, 0, 0)),
            pl.BlockSpec((1, tm, d), row),
            pl.BlockSpec((1, tm, d), row),
            _const_spec(wba.shape), _const_spec(wmo.shape), _const_spec(gq.shape),
            _const_spec(wmq.shape),
            pl.BlockSpec((1,) + kmt.shape[1:], lambda bi, i: (bi, 0, 0, 0)),
            pl.BlockSpec((1,) + vm.shape[1:], lambda bi, i: (bi, 0, 0)),
            _const_spec(wmemo.shape),
        ],
        out_specs=pl.BlockSpec((1, tm, d), row),
        out_shape=jax.ShapeDtypeStruct((b, s, d), F32),
        compiler_params=pltpu.CompilerParams(
            dimension_semantics=("parallel", "parallel"), vmem_limit_bytes=VMEM_LIMIT),
        name="mix_out",
    )(x, oa, sa, mb, wba, wmo, gq, wmq, kmt, vm, wmemo)


def _ffn_kernel(x_ref, gf_ref, win_ref, wo_ref, gfin_ref, o_ref, *, chunks, final_norm):
    hidden = wo_ref.shape[0]
    tiles = pl.cdiv(hidden, MXU_DIM)
    bounds = [min(hidden, (c * tiles // chunks) * MXU_DIM) for c in range(chunks + 1)]
    x = x_ref[0]
    xg = (x * gf_ref[...]).astype(BF16)
    r = _rms_scale(x)
    acts = []
    for c in range(chunks):
        lo, hi = bounds[c], bounds[c + 1]
        gate = _dot(xg, _bf16(win_ref[:, lo:hi])) * r
        up = _dot(xg, _bf16(win_ref[:, hidden + lo:hidden + hi])) * r
        acts.append((gate * jax.nn.sigmoid(gate) * up).astype(BF16))
    y = x
    for c in range(chunks - 1):
        y = y + _dot(acts[c], _bf16(wo_ref[bounds[c]:bounds[c + 1], :]))
    w_last = _bf16(wo_ref[bounds[chunks - 1]:bounds[chunks], :])
    tm = x.shape[0]
    for g in range(FFN_TAIL_GROUPS):
        rows = slice(g * tm // FFN_TAIL_GROUPS, (g + 1) * tm // FFN_TAIL_GROUPS)
        y_g = y[rows] + _dot(acts[chunks - 1][rows], w_last)
        o_ref[0, rows] = _rms(y_g, gfin_ref[...]) if final_norm else y_g


def _ffn(x, gf, win, wo, gfin, final_norm):
    b, s, d = x.shape
    tm = FFN_TOKEN_TILE
    row = lambda bi, i: (bi, i, 0)
    chunks = 4
    assert win.shape[1] == 2 * wo.shape[0] and wo.shape[0] % LANES == 0
    return pl.pallas_call(
        functools.partial(_ffn_kernel, chunks=chunks, final_norm=final_norm),
        grid=(b, s // tm),
        in_specs=[pl.BlockSpec((1, tm, d), row), _const_spec(gf.shape), _const_spec(win.shape),
                  _const_spec(wo.shape), _const_spec(gfin.shape)],
        out_specs=pl.BlockSpec((1, tm, d), row),
        out_shape=jax.ShapeDtypeStruct((b, s, d), F32),
        compiler_params=pltpu.CompilerParams(
            dimension_semantics=("parallel", "parallel"), vmem_limit_bytes=VMEM_LIMIT),
        name="ffn",
    )(x, gf, win, wo, gfin)


def kernel(x, mem, norm_mix, w_in, conv_w, w_branch_a, w_branch_b, w_mix_out, norm_mem_q,
           norm_mem_kv, w_mem_q, w_mem_kv, w_mem_o, norm_ffn, w_ffn_in, w_ffn_out, norm_final):
    depth = w_in.shape[0]
    assert x.shape[1] % MIX_TOKEN_TILE == 0 and x.shape[1] % FFN_TOKEN_TILE == 0
    assert MIX_TOKEN_TILE % SB_TILE == 0
    assert LANES == 2 * SB_HEAD_DIM

    idx = jnp.arange(SB_TILE)
    tri = (idx[None, :] >= idx[:, None]).astype(BF16)

    for l in range(depth):
        qt, k, vt, sa, mb = _mix_in(x, norm_mix[l][None, :], w_in[l], conv_w[l], w_branch_b[l])
        oa = _sb_attn(qt, k, vt, tri)
        kmt, vm = _mem_kv(mem, norm_mem_kv[l][None, :], w_mem_kv[l])
        x = _mix_out(x, oa, sa, mb, w_branch_a[l], w_mix_out[l], norm_mem_q[l][None, :],
                     w_mem_q[l], kmt, vm, w_mem_o[l])
        x = _ffn(x, norm_ffn[l][None, :], w_ffn_in[l], w_ffn_out[l],
                 norm_final[None, :], final_norm=(l == depth - 1))
    return x
```

```python
import functools
import math

import jax
import jax.numpy as jnp
from jax import lax
from jax.experimental import pallas as pl
from jax.experimental.pallas import tpu as pltpu

EPS = 1e-6
SB_HEADS = 8
SB_HEAD_DIM = 64
CONV_K = 3
MEM_HEADS = 4

LANES = 128
SUBLANES = 8
MXU_DIM = 256
FFN_TOKEN_TILE = 1024
FFN_TAIL_GROUPS = 4
MIX_TOKEN_TILE = 1024
SB_TILE = 256
MIB = 1024 * 1024
VMEM_LIMIT = 60 * MIB

LOG2E = math.log2(math.e)
MASKED = -1e30
DEAD_LOG2 = -160.0

BF16 = jnp.bfloat16
F32 = jnp.float32


def _dot(a, b):
    return jnp.dot(a, b, preferred_element_type=F32)


def _dot_tn(a, b):
    return lax.dot_general(a, b, (((0,), (1,)), ((), ())), preferred_element_type=F32)


def _bf16(w):
    return w.astype(BF16)


def _sigmoid(x):
    return 0.5 * jnp.tanh(0.5 * x) + 0.5


def _rms_scale(x):
    return lax.rsqrt(jnp.mean(x * x, axis=-1, keepdims=True) + EPS)


def _rms(x, g):
    return x * _rms_scale(x) * g


def _const_spec(shape):
    zeros = (0,) * len(shape)
    return pl.BlockSpec(shape, lambda *_: zeros, pipeline_mode=pl.Buffered(1))


def _mix_in_kernel(x_ref, g_ref, win_ref, cw_ref, wbb_ref,
                   qt_ref, k_ref, vt_ref, sa_ref, mb_ref, cu_buf):
    tm = x_ref.shape[1]
    d = x_ref.shape[2]
    cw = wbb_ref.shape[0]
    sbw = k_ref.shape[2]
    halo = SUBLANES
    o_conv = 3 * sbw
    o_ga = o_conv + 3 * cw
    o_gb = o_ga + d

    @pl.when(pl.program_id(1) == 0)
    def _():
        cu_buf[0:halo, :] = jnp.zeros((halo, cw), F32)

    h = _rms(x_ref[0], g_ref[...]).astype(BF16)

    ugc = _dot(h, _bf16(win_ref[:, o_conv:o_ga]))
    cu = ugc[:, 2 * cw:3 * cw] * ugc[:, 0:cw]
    gate_b = ugc[:, cw:2 * cw]
    cu_buf[halo:halo + tm, :] = cu
    conv = cw_ref[2:3, :] * cu
    for i in range(CONV_K - 1):
        shift = CONV_K - 1 - i
        conv = conv + cw_ref[i:i + 1, :] * cu_buf[halo - shift:halo - shift + tm, :]
    cu_buf[0:halo, :] = cu[tm - halo:tm, :]
    yb = (gate_b * conv).astype(BF16)

    sa_ref[0] = _sigmoid(_dot(h, _bf16(win_ref[:, o_ga:o_gb]))).astype(BF16)
    sig_b = _sigmoid(_dot(h, _bf16(win_ref[:, o_gb:o_gb + d])))
    mb_ref[0] = (sig_b * _dot(yb, _bf16(wbb_ref[...]))).astype(BF16)

    qt = _dot_tn(_bf16(win_ref[:, 0:sbw]), h) * (LOG2E * SB_HEAD_DIM ** -0.5)
    qt = qt.astype(BF16)
    vt = _dot_tn(_bf16(win_ref[:, 2 * sbw:3 * sbw]), h).astype(BF16)
    for p in range(qt_ref.shape[1]):
        for j in range(qt_ref.shape[2]):
            rows, cols = slice(p * LANES, (p + 1) * LANES), slice(j * SB_TILE, (j + 1) * SB_TILE)
            qt_ref[0, p, j] = qt[rows, cols]
            vt_ref[0, p, j] = vt[rows, cols]
    k_ref[0] = _dot(h, _bf16(win_ref[:, sbw:2 * sbw])).astype(BF16)


def _mix_in(x, g, win, cw, wbb):
    b, s, d = x.shape
    tm = MIX_TOKEN_TILE
    sbw = SB_HEADS * SB_HEAD_DIM
    cwid = wbb.shape[0]
    assert win.shape[1] == 3 * sbw + 3 * cwid + 2 * d
    pairs = sbw // LANES
    tiled = jax.ShapeDtypeStruct((b, pairs, s // SB_TILE, LANES, SB_TILE), BF16)
    out_shape = (
        tiled,
        jax.ShapeDtypeStruct((b, s, sbw), BF16),
        tiled,
        jax.ShapeDtypeStruct((b, s, d), BF16),
        jax.ShapeDtypeStruct((b, s, d), BF16),
    )
    row = lambda bi, i: (bi, i, 0)
    tiled_spec = pl.BlockSpec((1, pairs, tm // SB_TILE, LANES, SB_TILE),
                              lambda bi, i: (bi, 0, i, 0, 0))
    return pl.pallas_call(
        _mix_in_kernel,
        grid=(b, s // tm),
        in_specs=[
            pl.BlockSpec((1, tm, d), row),
            _const_spec(g.shape), _const_spec(win.shape), _const_spec(cw.shape),
            _const_spec(wbb.shape),
        ],
        out_specs=(
            tiled_spec,
            pl.BlockSpec((1, tm, sbw), row),
            tiled_spec,
            pl.BlockSpec((1, tm, d), row),
            pl.BlockSpec((1, tm, d), row),
        ),
        out_shape=out_shape,
        scratch_shapes=[pltpu.VMEM((tm + SUBLANES, cwid), F32)],
        compiler_params=pltpu.CompilerParams(
            dimension_semantics=("arbitrary", "arbitrary"),
            vmem_limit_bytes=VMEM_LIMIT),
        name="mix_in",
    )(x, g, win, cw, wbb)


SB_STAGES = 5
SB_DRAIN = SB_STAGES - 1
SB_SLOTS = 4
SB_LONG_BLOCK = 16
SB_SHORT_BLOCK = 4


def _sb_attn_kernel(qt_ref, k_ref, vt_ref, tri_ref, o_ref,
                    qm_ref, c_ref, acc_ref, z_buf, p_buf, cum_buf, w_buf):
    nq, tq = qt_ref.shape[2], qt_ref.shape[4]
    tk = SB_TILE
    hd = SB_HEAD_DIM
    dummy = nq

    def prep(i, carry):
        qt = qt_ref[0, 0, i]
        zero = jnp.zeros((hd, tq), BF16)
        qm_ref[0, i, 0:hd, :] = qt[0:hd]
        qm_ref[0, i, hd:2 * hd, :] = zero
        qm_ref[1, i, 0:hd, :] = zero
        qm_ref[1, i, hd:2 * hd, :] = qt[hd:2 * hd]
        return carry

    lax.fori_loop(0, nq, prep, 0)
    qm_ref[:, dummy] = jnp.zeros((2, 2 * hd, tq), BF16)
    @pl.when(jnp.logical_and(pl.program_id(0) == 0, pl.program_id(1) == 0))
    def _():
        z_buf[...] = jnp.zeros(z_buf.shape, F32)
        p_buf[...] = jnp.zeros(p_buf.shape, BF16)
        cum_buf[...] = jnp.zeros(cum_buf.shape, F32)
        w_buf[...] = jnp.zeros(w_buf.shape, BF16)

    key = lax.broadcasted_iota(jnp.int32, (tk, tq), 0)
    qry = lax.broadcasted_iota(jnp.int32, (tk, tq), 1)
    causal = key < qry

    def stage_ma(i, j, h, slot):
        kt = k_ref[0, pl.ds(pl.multiple_of(j * tk, tk), tk), :]
        z_buf[slot] = _dot(kt, qm_ref[h, i])

    def softplus2(z):
        p = jnp.maximum(z, 0.0) + jnp.log(1.0 + jnp.exp2(-jnp.abs(z))) * LOG2E
        return p.astype(BF16)

    def stage_ea(slot, masked):
        z = z_buf[slot]
        if masked:
            z = jnp.where(causal, z, MASKED)
            z_buf[slot] = z
        p_buf[slot] = softplus2(z)

    def stage_mb(slot):
        cum_buf[slot] = _dot(tri_ref[...], p_buf[slot])

    def stage_eb(i, h, slot, masked):
        cum = cum_buf[slot]
        if masked:
            w_buf[slot] = jnp.exp2(z_buf[slot] - cum).astype(BF16)
            c_ref[h, i] = jnp.broadcast_to(-cum[0:1, :], (SUBLANES, tq))
        else:
            c = c_ref[h, i, 0:1, :]
            w_buf[slot] = jnp.exp2(z_buf[slot] - cum + c).astype(BF16)
            c_ref[h, i] = jnp.broadcast_to(c - cum[0:1, :], (SUBLANES, tq))

    def stage_mc(i, j, h, slot, masked):
        vt = vt_ref[0, 0, j, h * hd:(h + 1) * hd, :]
        out = _dot(vt, w_buf[slot])
        if masked:
            acc_ref[h, i] = out
        else:
            acc_ref[h, i] += out

    def block(item, diagonal, size, first_step=0):
        def body(u, carry):
            for r in range(size):
                s = first_step + size * u + r

                def in_stage(k):
                    return item(s - k), (r - k) % 2, (r - k) % SB_SLOTS

                (i4, j4), h4, slot4 = in_stage(4)
                stage_mc(i4, j4, h4, slot4, diagonal(4, r))
                (i3, _), h3, slot3 = in_stage(3)
                stage_eb(i3, h3, slot3, diagonal(3, r))
                stage_mb(in_stage(2)[2])
                stage_ea(in_stage(1)[2], diagonal(1, r))
                (i0, j0), h0, slot0 = in_stage(0)
                stage_ma(i0, j0, h0, slot0)
            return carry
        return body

    def run_plain(item, first_step, last_step):
        never = lambda k, r: False
        n_long = (last_step - first_step) // SB_LONG_BLOCK
        lax.fori_loop(0, n_long, block(item, never, SB_LONG_BLOCK, first_step), 0)
        rest = first_step + SB_LONG_BLOCK * n_long
        n_short = (last_step - rest + SB_SHORT_BLOCK - 1) // SB_SHORT_BLOCK
        lax.fori_loop(0, n_short, block(item, never, SB_SHORT_BLOCK, rest), 0)

    def run_head():
        n_diag, n_next = 2 * nq, 2 * (nq - 1)

        def item(m):
            t0, t1 = m // 2, (m - n_diag) // 2
            in_diag = jnp.logical_and(m >= 0, m < n_diag)
            in_next = jnp.logical_and(m >= n_diag, m < n_diag + n_next)
            i = jnp.where(in_diag, t0, jnp.where(in_next, t1 + 1, dummy))
            j = jnp.where(in_diag, t0, jnp.where(in_next, t1, 0))
            return i, j

        lax.fori_loop(0, n_diag // SB_LONG_BLOCK,
                      block(item, lambda k, r: True, SB_LONG_BLOCK), 0)
        block(item, lambda k, r: r < k, SB_LONG_BLOCK, n_diag)(0, 0)
        run_plain(item, n_diag + SB_LONG_BLOCK, n_diag + n_next + SB_DRAIN)

    def run_distance(d):
        n_items = 2 * (nq - d)

        def item(m):
            valid = jnp.logical_and(m >= 0, m < n_items)
            t = m // 2
            return jnp.where(valid, d + t, dummy), jnp.where(valid, t, 0)

        run_plain(item, 0, n_items + SB_DRAIN)

    def alive_after(d):
        cc = jnp.maximum(c_ref[0, 0:nq], c_ref[1, 0:nq])
        blk = lax.broadcasted_iota(jnp.int32, cc.shape, 0)
        top = jnp.max(jnp.where(blk > d, cc, MASKED), axis=0)
        return (jnp.max(top) > DEAD_LOG2).astype(jnp.int32)

    run_head()

    def more(state):
        d, alive = state
        return jnp.logical_and(d < nq, alive > 0)

    def step(state):
        d, _ = state
        run_distance(d)
        return d + 1, alive_after(d)

    lax.while_loop(more, step, (jnp.int32(2), alive_after(1)))

    for h in range(2):
        o_ref[0, 0, :, h * hd:(h + 1) * hd, :] = acc_ref[h, 0:nq].astype(BF16)


def _sb_attn(qt, k, vt, tri):
    b, pairs, nq, _, tq = qt.shape
    s, sbw = k.shape[1], k.shape[2]
    tk = SB_TILE
    assert (2 * nq) % SB_LONG_BLOCK == 0 and 2 * nq - 2 > SB_DRAIN <= SB_LONG_BLOCK
    assert 2 * (nq - 1) + SB_DRAIN >= SB_LONG_BLOCK
    return pl.pallas_call(
        _sb_attn_kernel,
        grid=(b, pairs),
        in_specs=[
            pl.BlockSpec((1, 1, nq, LANES, tq), lambda bi, p: (bi, p, 0, 0, 0)),
            pl.BlockSpec((1, s, LANES), lambda bi, p: (bi, 0, p)),
            pl.BlockSpec((1, 1, s // tk, LANES, tk), lambda bi, p: (bi, p, 0, 0, 0)),
            _const_spec(tri.shape),
        ],
        out_specs=pl.BlockSpec((1, 1, nq, LANES, tq), lambda bi, p: (bi, p, 0, 0, 0)),
        out_shape=jax.ShapeDtypeStruct(qt.shape, BF16),
        scratch_shapes=[
            pltpu.VMEM((2, nq + 1, LANES, tq), BF16),
            pltpu.VMEM((2, nq + 1, SUBLANES, tq), F32),
            pltpu.VMEM((2, nq + 1, SB_HEAD_DIM, tq), F32),
            pltpu.VMEM((SB_SLOTS, tk, tq), F32),
            pltpu.VMEM((SB_SLOTS, tk, tq), BF16),
            pltpu.VMEM((SB_SLOTS, tk, tq), F32),
            pltpu.VMEM((SB_SLOTS, tk, tq), BF16),
        ],
        compiler_params=pltpu.CompilerParams(
            dimension_semantics=("arbitrary", "arbitrary"),
            vmem_limit_bytes=VMEM_LIMIT),
        name="sb_attn",
    )(qt, k, vt, tri)


def _mem_kv_kernel(mem_ref, g_ref, wkv_ref, kt_ref, v_ref):
    nb, m, d = mem_ref.shape
    hm = _rms(mem_ref[...].reshape(nb * m, d), g_ref[...]).astype(BF16)
    kt = _dot_tn(_bf16(wkv_ref[:, 0:d]), hm).astype(BF16)
    v = _dot(hm, _bf16(wkv_ref[:, d:2 * d])).astype(BF16)
    dh = kt_ref.shape[2]
    for bi in range(nb):
        for hd in range(kt_ref.shape[1]):
            kt_ref[bi, hd] = kt[hd * dh:(hd + 1) * dh, bi * m:(bi + 1) * m]
        v_ref[bi] = v[bi * m:(bi + 1) * m]


def _mem_kv(mem, g, wkv):
    b, m, d = mem.shape
    dh = d // MEM_HEADS
    kt_shape, v_shape = (b, MEM_HEADS, dh, m), (b, m, d)
    return pl.pallas_call(
        _mem_kv_kernel,
        grid=(1,),
        in_specs=[_const_spec(mem.shape), _const_spec(g.shape), _const_spec(wkv.shape)],
        out_specs=(pl.BlockSpec(kt_shape, lambda i: (0, 0, 0, 0)),
                   pl.BlockSpec(v_shape, lambda i: (0, 0, 0))),
        out_shape=(jax.ShapeDtypeStruct(kt_shape, BF16), jax.ShapeDtypeStruct(v_shape, BF16)),
        compiler_params=pltpu.CompilerParams(
            dimension_semantics=("arbitrary",), vmem_limit_bytes=VMEM_LIMIT),
        name="mem_kv",
    )(mem, g, wkv)


def _mix_out_kernel(x_ref, oa_ref, sa_ref, mb_ref, wba_ref, wmo_ref, gq_ref, wmq_ref,
                    kmt_ref, vm_ref, wmemo_ref, o_ref):
    w_a = _bf16(wba_ref[...])
    branch_a = jnp.concatenate(
        [lax.dot_general(
            jnp.concatenate([oa_ref[0, p, j] for p in range(oa_ref.shape[1])], axis=0),
            w_a, (((0,), (0,)), ((), ())), preferred_element_type=F32)
         for j in range(oa_ref.shape[2])], axis=0)
    merged = (sa_ref[0].astype(F32) * branch_a + mb_ref[0].astype(F32)).astype(BF16)
    x1 = x_ref[0] + _dot(merged, _bf16(wmo_ref[...]))

    dh = kmt_ref.shape[2]
    xg = (x1 * gq_ref[...]).astype(BF16)
    qm = (_dot(xg, _bf16(wmq_ref[...])) * (_rms_scale(x1) * (dh ** -0.5))).astype(BF16)
    cols = [slice(hd * dh, (hd + 1) * dh) for hd in range(MEM_HEADS)]
    scores = [_dot(qm[:, c], kmt_ref[0, hd]) for hd, c in enumerate(cols)]
    heads = []
    for sc, c in zip(scores, cols):
        e = jnp.exp(sc - jnp.max(sc, axis=-1, keepdims=True))
        denom = jnp.sum(e, axis=-1, keepdims=True)
        o = _dot(e.astype(BF16), vm_ref[0, :, c]) / denom
        heads.append(o.astype(BF16))
    o_ref[0] = x1 + _dot(jnp.concatenate(heads, axis=1), _bf16(wmemo_ref[...]))


def _mix_out(x, oa, sa, mb, wba, wmo, gq, wmq, kmt, vm, wmemo):
    b, s, d = x.shape
    tm = MIX_TOKEN_TILE
    row = lambda bi, i: (bi, i, 0)
    return pl.pallas_call(
        _mix_out_kernel,
        grid=(b, s // tm),
        in_specs=[
            pl.BlockSpec((1, tm, d), row),
            pl.BlockSpec((1, oa.shape[1], tm // oa.shape[4], LANES, oa.shape[4]),
                         lambda bi, i: (bi, 0, i, 0, 0)),
            pl.BlockSpec((1, tm, d), row),
            pl.BlockSpec((1, tm, d), row),
            _const_spec(wba.shape), _const_spec(wmo.shape), _const_spec(gq.shape),
            _const_spec(wmq.shape),
            pl.BlockSpec((1,) + kmt.shape[1:], lambda bi, i: (bi, 0, 0, 0)),
            pl.BlockSpec((1,) + vm.shape[1:], lambda bi, i: (bi, 0, 0)),
            _const_spec(wmemo.shape),
        ],
        out_specs=pl.BlockSpec((1, tm, d), row),
        out_shape=jax.ShapeDtypeStruct((b, s, d), F32),
        compiler_params=pltpu.CompilerParams(
            dimension_semantics=("parallel", "parallel"), vmem_limit_bytes=VMEM_LIMIT),
        name="mix_out",
    )(x, oa, sa, mb, wba, wmo, gq, wmq, kmt, vm, wmemo)


def _ffn_kernel(x_ref, gf_ref, win_ref, wo_ref, gfin_ref, o_ref, *, chunks, final_norm):
    hidden = wo_ref.shape[0]
    tiles = pl.cdiv(hidden, MXU_DIM)
    bounds = [min(hidden, (c * tiles // chunks) * MXU_DIM) for c in range(chunks + 1)]
    x = x_ref[0]
    xg = (x * gf_ref[...]).astype(BF16)
    r = _rms_scale(x)
    acts = []
    for c in range(chunks):
        lo, hi = bounds[c], bounds[c + 1]
        gate = _dot(xg, _bf16(win_ref[:, lo:hi])) * r
        up = _dot(xg, _bf16(win_ref[:, hidden + lo:hidden + hi])) * r
        acts.append((gate * _sigmoid(gate) * up).astype(BF16))
    y = x
    for c in range(chunks - 1):
        y = y + _dot(acts[c], _bf16(wo_ref[bounds[c]:bounds[c + 1], :]))
    w_last = _bf16(wo_ref[bounds[chunks - 1]:bounds[chunks], :])
    tm = x.shape[0]
    for g in range(FFN_TAIL_GROUPS):
        rows = slice(g * tm // FFN_TAIL_GROUPS, (g + 1) * tm // FFN_TAIL_GROUPS)
        y_g = y[rows] + _dot(acts[chunks - 1][rows], w_last)
        o_ref[0, rows] = _rms(y_g, gfin_ref[...]) if final_norm else y_g


def _ffn(x, gf, win, wo, gfin, final_norm):
    b, s, d = x.shape
    tm = FFN_TOKEN_TILE
    row = lambda bi, i: (bi, i, 0)
    chunks = 4
    assert win.shape[1] == 2 * wo.shape[0] and wo.shape[0] % LANES == 0
    return pl.pallas_call(
        functools.partial(_ffn_kernel, chunks=chunks, final_norm=final_norm),
        grid=(b, s // tm),
        in_specs=[pl.BlockSpec((1, tm, d), row), _const_spec(gf.shape), _const_spec(win.shape),
                  _const_spec(wo.shape), _const_spec(gfin.shape)],
        out_specs=pl.BlockSpec((1, tm, d), row),
        out_shape=jax.ShapeDtypeStruct((b, s, d), F32),
        compiler_params=pltpu.CompilerParams(
            dimension_semantics=("parallel", "parallel"), vmem_limit_bytes=VMEM_LIMIT),
        name="ffn",
    )(x, gf, win, wo, gfin)


def kernel(x, mem, norm_mix, w_in, conv_w, w_branch_a, w_branch_b, w_mix_out, norm_mem_q,
           norm_mem_kv, w_mem_q, w_mem_kv, w_mem_o, norm_ffn, w_ffn_in, w_ffn_out, norm_final):
    depth = w_in.shape[0]
    assert x.shape[1] % MIX_TOKEN_TILE == 0 and x.shape[1] % FFN_TOKEN_TILE == 0
    assert MIX_TOKEN_TILE % SB_TILE == 0
    assert LANES == 2 * SB_HEAD_DIM

    idx = jnp.arange(SB_TILE)
    tri = (idx[None, :] >= idx[:, None]).astype(BF16)

    for l in range(depth):
        qt, k, vt, sa, mb = _mix_in(x, norm_mix[l][None, :], w_in[l], conv_w[l], w_branch_b[l])
        oa = _sb_attn(qt, k, vt, tri)
        kmt, vm = _mem_kv(mem, norm_mem_kv[l][None, :], w_mem_kv[l])
        x = _mix_out(x, oa, sa, mb, w_branch_a[l], w_mix_out[l], norm_mem_q[l][None, :],
                     w_mem_q[l], kmt, vm, w_mem_o[l])
        x = _ffn(x, norm_ffn[l][None, :], w_ffn_in[l], w_ffn_out[l],
                 norm_final[None, :], final_norm=(l == depth - 1))
    return x
```
